```python
import jax, jax.numpy as jnp
from jax import lax
import numpy as np

D_MODEL = 2048
BATCH = 4
SEQ = 4096
DEPTH = 2

N_META = 16
N_MIXERS = 2
N_HEADS = 16
HEAD_DIM = D_MODEL // N_HEADS
ATTN_WIDTH = N_HEADS * HEAD_DIM
FOX_IN_COLS = 4 * ATTN_WIDTH + N_HEADS
Q_BLOCK = 128
POOL_WINDOWS = (2, 4, 8, 16)
N_POOL_GROUPS = len(POOL_WINDOWS)
POOL_WIDTH = D_MODEL
POOL_GROUP = POOL_WIDTH // N_POOL_GROUPS
POOL_IN_COLS = 2 * POOL_WIDTH
ALPHA = (2.0 * DEPTH) ** 0.25
BETA = (8.0 * DEPTH) ** -0.25
LN_EPS = 1e-5

kernel_name = "fox_pool_interleaved_deepnorm"


def layer_norm(x, g, b):
    xf = x.astype(jnp.float32)
    mu = jnp.mean(xf, axis=-1, keepdims=True)
    var = jnp.mean(jnp.square(xf - mu), axis=-1, keepdims=True)
    y = (xf - mu) * lax.rsqrt(var + LN_EPS)
    return (y * g.astype(jnp.float32) + b.astype(jnp.float32)).astype(x.dtype)


def _fox_attend(q_blk, c_q, q_pos, k, v, c_k, k_pos):
    s = jnp.einsum('bhqd,bhkd->bhqk', q_blk, k).astype(jnp.float32) * (HEAD_DIM ** -0.5)
    s = s + c_q[..., :, None] - c_k[..., None, :]
    mask = k_pos[None, :] <= q_pos[:, None]
    s = jnp.where(mask[None, None], s, -jnp.inf)
    p = jax.nn.softmax(s, axis=-1)
    return jnp.einsum('bhqk,bhkd->bhqd', p.astype(v.dtype), v)


def fox_mixer(h, w_in, b_f, w_out):
    B, L, _ = h.shape
    proj = h @ w_in
    q, k, v, z, f = jnp.split(proj, [ATTN_WIDTH, 2 * ATTN_WIDTH, 3 * ATTN_WIDTH, 4 * ATTN_WIDTH], axis=-1)
    to_heads = lambda t: t.reshape(B, L, N_HEADS, HEAD_DIM).transpose(0, 2, 1, 3)
    q, k, v = to_heads(q), to_heads(k), to_heads(v)
    log_f = jax.nn.log_sigmoid(f.astype(jnp.float32) + b_f.astype(jnp.float32))
    c = jnp.cumsum(log_f, axis=1).transpose(0, 2, 1)
    k_pos = jnp.arange(L)
    o_meta = _fox_attend(q[:, :, :N_META], c[:, :, :N_META], k_pos[:N_META],
                         k[:, :, :N_META], v[:, :, :N_META], c[:, :, :N_META], k_pos[:N_META])
    n_blk = (L - N_META) // Q_BLOCK
    q_r = q[:, :, N_META:].reshape(B, N_HEADS, n_blk, Q_BLOCK, HEAD_DIM).transpose(2, 0, 1, 3, 4)
    c_r = c[:, :, N_META:].reshape(B, N_HEADS, n_blk, Q_BLOCK).transpose(2, 0, 1, 3)

    def block(args):
        q_blk, c_blk, idx = args
        q_pos = N_META + idx * Q_BLOCK + jnp.arange(Q_BLOCK)
        return _fox_attend(q_blk, c_blk, q_pos, k, v, c, k_pos)

    o_r = lax.map(block, (q_r, c_r, jnp.arange(n_blk)))
    o_r = o_r.transpose(1, 2, 0, 3, 4).reshape(B, N_HEADS, n_blk * Q_BLOCK, HEAD_DIM)
    o = jnp.concatenate([o_meta, o_r], axis=2).transpose(0, 2, 1, 3).reshape(B, L, ATTN_WIDTH)
    return (o * jax.nn.silu(z)) @ w_out


def pool_mixer(h, w_in, w_grp, scale, w_out):
    B, L, _ = h.shape
    proj = h @ w_in
    u, z = jnp.split(proj, [POOL_WIDTH], axis=-1)
    uf = u.astype(jnp.float32).reshape(B, L, N_POOL_GROUPS, POOL_GROUP)
    t1 = jnp.arange(1, L + 1, dtype=jnp.float32)
    outs = []
    for g, w in enumerate(POOL_WINDOWS):
        ug = uf[:, :, g]
        cs = jnp.pad(jnp.cumsum(ug, axis=1), ((0, 0), (w, 0), (0, 0)))
        win_sum = cs[:, w:] - cs[:, :L]
        cnt = jnp.minimum(t1, float(w))[None, :, None]
        outs.append(win_sum / cnt - ug)
    d = jnp.stack(outs, axis=2)
    d = jnp.einsum('blgc,gce->blge', d, w_grp.astype(jnp.float32)).reshape(B, L, POOL_WIDTH)
    d = (d * scale.astype(jnp.float32)).astype(z.dtype)
    return (d * jax.nn.silu(z)) @ w_out


def setup_inputs(seed: int = 0) -> dict:
    key = jax.random.key(seed)
    ks = jax.random.split(key, 14)
    n = jax.random.normal
    f32 = jnp.float32
    return {
        "x": n(ks[0], (BATCH, SEQ, D_MODEL), f32),
        "meta_tokens": n(ks[1], (N_META, D_MODEL), f32),
        "fox_w_in": n(ks[2], (D_MODEL, FOX_IN_COLS), f32) * D_MODEL ** -0.5,
        "fox_b_f": 2.0 + 0.5 * n(ks[3], (N_HEADS,), f32),
        "fox_w_out": n(ks[4], (ATTN_WIDTH, D_MODEL), f32) * ATTN_WIDTH ** -0.5 * BETA,
        "ln0_g": 1.0 + 0.02 * n(ks[5], (D_MODEL,), f32),
        "ln0_b": 0.02 * n(ks[6], (D_MODEL,), f32),
        "pool_w_in": n(ks[7], (D_MODEL, POOL_IN_COLS), f32) * D_MODEL ** -0.5,
        "pool_w_grp": n(ks[8], (N_POOL_GROUPS, POOL_GROUP, POOL_GROUP), f32) * POOL_GROUP ** -0.5,
        "pool_scale": 1.0 + 0.1 * n(ks[9], (POOL_WIDTH,), f32),
        "pool_w_out": n(ks[10], (POOL_WIDTH, D_MODEL), f32) * POOL_WIDTH ** -0.5 * BETA,
        "ln1_g": 1.0 + 0.02 * n(ks[11], (D_MODEL,), f32),
        "ln1_b": 0.02 * n(ks[12], (D_MODEL,), f32),
    }


def reference(x, meta_tokens, fox_w_in, fox_b_f, fox_w_out, ln0_g, ln0_b,
              pool_w_in, pool_w_grp, pool_scale, pool_w_out, ln1_g, ln1_b):
    B = x.shape[0]
    meta = jnp.broadcast_to(meta_tokens[None].astype(x.dtype), (B, N_META, D_MODEL))
    h = jnp.concatenate([meta, x], axis=1)
    norms = ((ln0_g, ln0_b), (ln1_g, ln1_b))
    for i in range(DEPTH):
        if i % N_MIXERS == 0:
            y = fox_mixer(h, fox_w_in, fox_b_f, fox_w_out)
        else:
            y = pool_mixer(h, pool_w_in, pool_w_grp, pool_scale, pool_w_out)
        g, b = norms[i]
        h = layer_norm(ALPHA * h + y, g, b)
    return h[:, N_META:]
```

```python
import functools

import jax
import jax.numpy as jnp
from jax import lax
from jax.experimental import pallas as pl
from jax.experimental.pallas import tpu as pltpu

N_META = 16
N_HEADS = 16
HEAD_DIM = 128
POOL_WINDOWS = (2, 4, 8, 16)
DEPTH = 2
ALPHA = (2.0 * DEPTH) ** 0.25
LN_EPS = 1e-5

LANES = 128
MASKED = -1e30
NO_KEY = 1e30
VMEM_LIMIT = 56 * 1024 * 1024

_NT = (((1,), (1,)), ((), ()))
_f32 = jnp.float32
_bf16 = jnp.bfloat16


def _params(*sem):
    return pltpu.CompilerParams(dimension_semantics=sem, vmem_limit_bytes=VMEM_LIMIT)


def _fox_proj_kernel(x_ref, w_ref, wf_ref, out_ref, flt_ref, xb_ref, *, n_q_tiles, q_scale):
    j = pl.program_id(1)

    @pl.when(j == 0)
    def _():
        xb = x_ref[...].astype(_bf16)
        xb_ref[...] = xb
        flt_ref[...] = lax.dot_general(wf_ref[...], xb, _NT, preferred_element_type=_f32)

    acc = jnp.dot(xb_ref[...], w_ref[...], preferred_element_type=_f32)
    scale = jnp.where(j < n_q_tiles, q_scale, 1.0).astype(_f32)
    out_ref[...] = (acc * scale).astype(out_ref.dtype)


def _fox_proj(x, w, wf_t, *, tm, tn):
    m, d = x.shape
    n = 4 * N_HEADS * HEAD_DIM
    kern = functools.partial(_fox_proj_kernel, n_q_tiles=(N_HEADS * HEAD_DIM) // tn,
                             q_scale=HEAD_DIM ** -0.5)
    return pl.pallas_call(
        kern,
        grid=(m // tm, n // tn),
        in_specs=[
            pl.BlockSpec((tm, d), lambda i, j: (i, 0)),
            pl.BlockSpec((d, tn), lambda i, j: (0, j)),
            pl.BlockSpec((N_HEADS, d), lambda i, j: (0, 0)),
        ],
        out_specs=[
            pl.BlockSpec((tm, tn), lambda i, j: (i, j)),
            pl.BlockSpec((N_HEADS, tm), lambda i, j: (0, i)),
        ],
        out_shape=[
            jax.ShapeDtypeStruct((m, n), _bf16),
            jax.ShapeDtypeStruct((N_HEADS, m), _f32),
        ],
        scratch_shapes=[pltpu.VMEM((tm, d), _bf16)],
        compiler_params=_params("arbitrary", "arbitrary"),
        name="fox_proj",
    )(x, w, wf_t)


def _cum_kernel(fl_ref, bf_ref, c_ref, *, n_pad):
    x = fl_ref[...] + bf_ref[...]
    lf = jnp.minimum(x, 0.0) - jnp.log1p(jnp.exp(-jnp.abs(x)))
    lane = lax.broadcasted_iota(jnp.int32, x.shape, 1)
    lf = jnp.where(lane >= n_pad, lf, 0.0)
    shift = 1
    while shift < x.shape[1]:
        lf = lf + jnp.where(lane >= shift, pltpu.roll(lf, shift, axis=1), 0.0)
        shift *= 2
    c_ref[...] = lf


def _cum(fl, b_f, *, n_pad):
    b, h, l = fl.shape
    return pl.pallas_call(
        functools.partial(_cum_kernel, n_pad=n_pad),
        grid=(b,),
        in_specs=[
            pl.BlockSpec((None, h, l), lambda i: (i, 0, 0)),
            pl.BlockSpec((h, 1), lambda i: (0, 0)),
        ],
        out_specs=pl.BlockSpec((None, h, l), lambda i: (i, 0, 0)),
        out_shape=jax.ShapeDtypeStruct((b, h, l), _f32),
        compiler_params=_params("arbitrary"),
        name="fox_decay_cumsum",
    )(fl, b_f)


def _online_softmax_step(s, v, m_sc, l_sc, acc_sc):
    m_prev = m_sc[...]
    m_new = jnp.maximum(m_prev, jnp.max(s, axis=1, keepdims=True))
    alpha = jnp.exp(m_prev - m_new)
    p = jnp.exp(s - m_new)
    l_sc[...] = alpha * l_sc[...] + jnp.sum(p, axis=1, keepdims=True)
    acc_sc[...] = alpha * acc_sc[...] + jnp.dot(p.astype(v.dtype), v, preferred_element_type=_f32)
    m_sc[...] = m_new


def _silu(z):
    return z * jax.nn.sigmoid(z)


def _attn_kernel(q_ref, k_ref, v_ref, z_ref, km_ref, vm_ref, c_ref, cm_ref, o_ref,
                 m_sc, l_sc, acc_sc, *, tile):
    qi = pl.program_id(2)
    q = q_ref[...]
    diag = pl.multiple_of(qi * tile, tile)
    c_diag = c_ref[:, pl.ds(diag, tile)]
    c_shift = jnp.min(c_diag, axis=1, keepdims=True)

    s = lax.dot_general(q, km_ref[...], _NT, preferred_element_type=_f32) + (c_shift - cm_ref[...])
    m = jnp.max(s, axis=1, keepdims=True)
    p = jnp.exp(s - m)
    m_sc[...] = m
    l_sc[...] = jnp.sum(p, axis=1, keepdims=True)
    acc_sc[...] = jnp.dot(p.astype(_bf16), vm_ref[...], preferred_element_type=_f32)

    def below_diagonal(ki, carry):
        start = pl.multiple_of(ki * tile, tile)
        k = k_ref[pl.ds(start, tile), :]
        s = lax.dot_general(q, k, _NT, preferred_element_type=_f32)
        s = s + (c_shift - c_ref[:, pl.ds(start, tile)])
        _online_softmax_step(s, v_ref[pl.ds(start, tile), :], m_sc, l_sc, acc_sc)
        return carry

    lax.fori_loop(0, qi, below_diagonal, 0)

    s = lax.dot_general(q, k_ref[pl.ds(diag, tile), :], _NT, preferred_element_type=_f32)
    s = s + (c_shift - c_diag)
    row = lax.broadcasted_iota(jnp.int32, s.shape, 0)
    col = lax.broadcasted_iota(jnp.int32, s.shape, 1)
    s = jnp.where(col <= row, s, MASKED)
    _online_softmax_step(s, v_ref[pl.ds(diag, tile), :], m_sc, l_sc, acc_sc)

    o = acc_sc[...] / l_sc[...]
    o_ref[...] = (o * _silu(z_ref[...].astype(_f32))).astype(o_ref.dtype)


def _attention(qkvz, qkvz_meta, c, c_meta, *, tile):
    b, l, _ = qkvz.shape
    h, dh = N_HEADS, HEAD_DIM
    return pl.pallas_call(
        functools.partial(_attn_kernel, tile=tile),
        grid=(b, h, l // tile),
        in_specs=[
            pl.BlockSpec((None, tile, dh), lambda bi, hi, qi: (bi, qi, hi)),
            pl.BlockSpec((None, l, dh), lambda bi, hi, qi: (bi, 0, h + hi)),
            pl.BlockSpec((None, l, dh), lambda bi, hi, qi: (bi, 0, 2 * h + hi)),
            pl.BlockSpec((None, tile, dh), lambda bi, hi, qi: (bi, qi, 3 * h + hi)),
            pl.BlockSpec((LANES, dh), lambda bi, hi, qi: (0, h + hi)),
            pl.BlockSpec((LANES, dh), lambda bi, hi, qi: (0, 2 * h + hi)),
            pl.BlockSpec((None, None, 1, l), lambda bi, hi, qi: (bi, hi, 0, 0)),
            pl.BlockSpec((None, 1, LANES), lambda bi, hi, qi: (hi, 0, 0)),
        ],
        out_specs=pl.BlockSpec((None, tile, dh), lambda bi, hi, qi: (bi, qi, hi)),
        out_shape=jax.ShapeDtypeStruct((b, l, h * dh), _bf16),
        scratch_shapes=[
            pltpu.VMEM((tile, 1), _f32),
            pltpu.VMEM((tile, 1), _f32),
            pltpu.VMEM((tile, dh), _f32),
        ],
        compiler_params=_params("arbitrary", "arbitrary", "arbitrary"),
        name="fox_attention",
    )(qkvz, qkvz, qkvz, qkvz, qkvz_meta, qkvz_meta, c, c_meta)


def _meta_attn_kernel(qkvz_ref, c_ref, o_ref):
    w = N_HEADS * HEAD_DIM
    row = lax.broadcasted_iota(jnp.int32, (N_META, LANES), 0)
    col = lax.broadcasted_iota(jnp.int32, (N_META, LANES), 1)
    for hd in range(N_HEADS):
        lo = hd * HEAD_DIM
        q = qkvz_ref[0:N_META, lo:lo + HEAD_DIM]
        k = qkvz_ref[:, w + lo:w + lo + HEAD_DIM]
        v = qkvz_ref[:, 2 * w + lo:2 * w + lo + HEAD_DIM]
        z = qkvz_ref[0:N_META, 3 * w + lo:3 * w + lo + HEAD_DIM]
        ck = c_ref[hd]
        c_shift = jnp.min(jnp.where(col[0:1] < N_META, ck, NO_KEY), axis=1, keepdims=True)
        s = lax.dot_general(q, k, _NT, preferred_element_type=_f32) + (c_shift - ck)
        s = jnp.where(col <= row, s, MASKED)
        m = jnp.max(s, axis=1, keepdims=True)
        p = jnp.exp(s - m)
        o = jnp.dot(p.astype(_bf16), v, preferred_element_type=_f32)
        o = o / jnp.sum(p, axis=1, keepdims=True)
        o_ref[:, lo:lo + HEAD_DIM] = (o * _silu(z.astype(_f32))).astype(o_ref.dtype)


def _meta_attention(qkvz_meta, c_meta):
    return pl.pallas_call(
        _meta_attn_kernel,
        out_shape=jax.ShapeDtypeStruct((N_META, N_HEADS * HEAD_DIM), _bf16),
        compiler_params=pltpu.CompilerParams(vmem_limit_bytes=VMEM_LIMIT),
        name="fox_meta_attention",
    )(qkvz_meta, c_meta)


def _layer_norm(r, g, b):
    mu = jnp.mean(r, axis=-1, keepdims=True)
    d = r - mu
    var = jnp.mean(d * d, axis=-1, keepdims=True)
    return d * lax.rsqrt(var + LN_EPS) * g + b


def _out_ln_kernel(a_ref, w_ref, res_ref, g_ref, b_ref, o_ref):
    y = jnp.dot(a_ref[...], w_ref[...], preferred_element_type=_f32)
    o_ref[...] = _layer_norm(ALPHA * res_ref[...] + y, g_ref[...], b_ref[...])


def _resident(shape):
    nd = len(shape)
    return pl.BlockSpec(shape, lambda *_: (0,) * nd, pipeline_mode=pl.Buffered(1))


def _out_ln(a, w, res, g, b, *, tm):
    m, d = res.shape
    k = a.shape[1]
    return pl.pallas_call(
        _out_ln_kernel,
        grid=(m // tm,),
        in_specs=[
            pl.BlockSpec((tm, k), lambda i: (i, 0)),
            _resident((k, d)),
            pl.BlockSpec((tm, d), lambda i: (i, 0)),
            _resident((1, d)),
            _resident((1, d)),
        ],
        out_specs=pl.BlockSpec((tm, d), lambda i: (i, 0)),
        out_shape=jax.ShapeDtypeStruct((m, d), _f32),
        compiler_params=_params("arbitrary"),
        name="out_proj_layernorm",
    )(a, w, res, g, b)


def _matmul_kernel(x_ref, w_ref, o_ref, xb_ref):
    @pl.when(pl.program_id(1) == 0)
    def _():
        xb_ref[...] = x_ref[...].astype(_bf16)

    o_ref[...] = jnp.dot(xb_ref[...], w_ref[...], preferred_element_type=_f32).astype(o_ref.dtype)


def _matmul(x, w, *, col0, n, out_dtype, tm, tn):
    m, d = x.shape
    off = col0 // tn
    return pl.pallas_call(
        _matmul_kernel,
        grid=(m // tm, n // tn),
        in_specs=[
            pl.BlockSpec((tm, d), lambda i, j: (i, 0)),
            pl.BlockSpec((d, tn), lambda i, j: (0, off + j)),
        ],
        out_specs=pl.BlockSpec((tm, tn), lambda i, j: (i, j)),
        out_shape=jax.ShapeDtypeStruct((m, n), out_dtype),
        scratch_shapes=[pltpu.VMEM((tm, d), _bf16)],
        compiler_params=_params("arbitrary", "arbitrary"),
        name="pool_in_proj",
    )(x, w)


def _pool_tail_kernel(u_ref, uprev_ref, umeta_ref, z_ref, h_ref, wg_ref, sc_ref, wo_ref,
                      g_ref, b_ref, o_ref, ext_ref, gate_ref, *, tm, tiles_per_seq):
    halo = max(POOL_WINDOWS)
    first = pl.program_id(0) % tiles_per_seq == 0
    ext_ref[0:halo, :] = jnp.where(first, umeta_ref[...], uprev_ref[...])
    ext_ref[halo:halo + tm, :] = u_ref[...]
    grp = u_ref.shape[1] // len(POOL_WINDOWS)
    for gi, win in enumerate(POOL_WINDOWS):
        cols = slice(gi * grp, (gi + 1) * grp)
        u = ext_ref[halo:halo + tm, cols]
        tot = u
        for back in range(1, win):
            tot = tot + ext_ref[halo - back:halo - back + tm, cols]
        d = tot / float(win) - u
        e = jnp.dot(d.astype(_bf16), wg_ref[gi], preferred_element_type=_f32) * sc_ref[:, cols]
        gate_ref[:, cols] = (e * _silu(z_ref[:, cols].astype(_f32))).astype(_bf16)
    y = jnp.dot(gate_ref[...], wo_ref[...], preferred_element_type=_f32)
    o_ref[...] = _layer_norm(ALPHA * h_ref[...] + y, g_ref[...], b_ref[...])


def _pool_tail(u, u_meta, z, h1, w_grp, scale, w_out, g, b, *, tm, seq):
    m, d = u.shape
    halo = max(POOL_WINDOWS)
    per_halo = tm // halo
    kern = functools.partial(_pool_tail_kernel, tm=tm, tiles_per_seq=seq // tm)
    return pl.pallas_call(
        kern,
        grid=(m // tm,),
        in_specs=[
            pl.BlockSpec((tm, d), lambda i: (i, 0)),
            pl.BlockSpec((halo, d), lambda i: (jnp.maximum(i * per_halo - 1, 0), 0)),
            _resident((halo, d)),
            pl.BlockSpec((tm, d), lambda i: (i, 0)),
            pl.BlockSpec((tm, d), lambda i: (i, 0)),
            _resident(w_grp.shape),
            _resident((1, d)),
            _resident((d, d)),
            _resident((1, d)),
            _resident((1, d)),
        ],
        out_specs=pl.BlockSpec((tm, d), lambda i: (i, 0)),
        out_shape=jax.ShapeDtypeStruct((m, d), _f32),
        scratch_shapes=[pltpu.VMEM((tm + halo, d), _f32), pltpu.VMEM((tm, d), _bf16)],
        compiler_params=_params("arbitrary"),
        name="pool_tail",
    )(u, u, u_meta, z, h1, w_grp, scale, w_out, g, b)


def kernel(x, meta_tokens, fox_w_in, fox_b_f, fox_w_out, ln0_g, ln0_b, pool_w_in, pool_w_grp,
           pool_scale, pool_w_out, ln1_g, ln1_b):
    bsz, seq, d = x.shape
    width = N_HEADS * HEAD_DIM
    m = bsz * seq

    w_in = fox_w_in.astype(_bf16)
    wf_t = fox_w_in[:, 4 * width:].T.astype(_bf16)
    w_out0 = fox_w_out.astype(_bf16)
    w_pin = pool_w_in.astype(_bf16)
    w_grp = pool_w_grp.astype(_bf16)
    w_out1 = pool_w_out.astype(_bf16)
    row = lambda t: t.reshape(1, d).astype(_f32)

    x2 = x.reshape(m, d)
    meta = meta_tokens.astype(_f32)

    qkvz, flt = _fox_proj(x2, w_in, wf_t, tm=1024, tn=1024)
    qkvz_m, flt_m = _fox_proj(meta, w_in, wf_t, tm=N_META, tn=1024)

    n_pad = LANES - N_META
    fl = jnp.concatenate(
        [jnp.zeros((bsz, N_HEADS, n_pad), _f32),
         jnp.broadcast_to(flt_m[None], (bsz, N_HEADS, N_META)),
         flt.reshape(N_HEADS, bsz, seq).transpose(1, 0, 2)], axis=2)
    c_full = _cum(fl, fox_b_f.reshape(N_HEADS, 1).astype(_f32), n_pad=n_pad)
    c_seq = c_full[:, :, LANES:].reshape(bsz, N_HEADS, 1, seq)
    c_meta = jnp.concatenate(
        [c_full[0, :, n_pad:LANES], jnp.full((N_HEADS, n_pad), NO_KEY, _f32)],
        axis=1).reshape(N_HEADS, 1, LANES)
    qkvz_m128 = jnp.pad(qkvz_m, ((0, LANES - N_META), (0, 0)))

    a = _attention(qkvz.reshape(bsz, seq, 4 * width), qkvz_m128, c_seq, c_meta, tile=512)
    a_m = _meta_attention(qkvz_m128, c_meta)

    g0, b0 = row(ln0_g), row(ln0_b)
    h1 = _out_ln(a.reshape(m, width), w_out0, x2, g0, b0, tm=512)
    h1_m = _out_ln(a_m, w_out0, meta, g0, b0, tm=N_META)

    u = _matmul(h1, w_pin, col0=0, n=d, out_dtype=_f32, tm=1024, tn=1024)
    z1 = _matmul(h1, w_pin, col0=d, n=d, out_dtype=_bf16, tm=1024, tn=1024)
    u_m = _matmul(h1_m, w_pin, col0=0, n=d, out_dtype=_f32, tm=N_META, tn=1024)
    out = _pool_tail(u, u_m, z1, h1, w_grp, row(pool_scale), w_out1, row(ln1_g), row(ln1_b),
                     tm=256, seq=seq)
    return out.reshape(bsz, seq, d)
```

```python
import functools
import math

import jax
import jax.numpy as jnp
from jax import lax
from jax.experimental import pallas as pl
from jax.experimental.pallas import tpu as pltpu

N_META = 16
N_HEADS = 16
HEAD_DIM = 128
POOL_WINDOWS = (2, 4, 8, 16)
DEPTH = 2
ALPHA = (2.0 * DEPTH) ** 0.25
LN_EPS = 1e-5

LANES = 128
MASKED = -1e30
NO_KEY = 1e30
VMEM_LIMIT = 56 * 1024 * 1024
LOG2E = math.log2(math.e)

_NT = (((1,), (1,)), ((), ()))
_TN = (((0,), (0,)), ((), ()))
_f32 = jnp.float32
_bf16 = jnp.bfloat16


def _params(*sem):
    return pltpu.CompilerParams(dimension_semantics=sem, vmem_limit_bytes=VMEM_LIMIT)


def _fox_proj_kernel(x_ref, w_ref, wf_ref, out_ref, flt_ref, xb_ref, *, n_q_tiles, q_scale):
    j = pl.program_id(1)

    @pl.when(j == 0)
    def _():
        xb = x_ref[...].astype(_bf16)
        xb_ref[...] = xb
        flt_ref[...] = lax.dot_general(wf_ref[...], xb, _NT, preferred_element_type=_f32)

    acc = jnp.dot(xb_ref[...], w_ref[...], preferred_element_type=_f32)
    scale = jnp.where(j < n_q_tiles, q_scale, 1.0).astype(_f32)
    out_ref[...] = (acc * scale).astype(out_ref.dtype)


def _fox_proj(x, w, wf_t, *, tm, tn):
    m, d = x.shape
    n = 4 * N_HEADS * HEAD_DIM
    kern = functools.partial(_fox_proj_kernel, n_q_tiles=(N_HEADS * HEAD_DIM) // tn,
                             q_scale=HEAD_DIM ** -0.5 * LOG2E)
    return pl.pallas_call(
        kern,
        grid=(m // tm, n // tn),
        in_specs=[
            pl.BlockSpec((tm, d), lambda i, j: (i, 0)),
            pl.BlockSpec((d, tn), lambda i, j: (0, j)),
            pl.BlockSpec((N_HEADS, d), lambda i, j: (0, 0)),
        ],
        out_specs=[
            pl.BlockSpec((tm, tn), lambda i, j: (i, j)),
            pl.BlockSpec((N_HEADS, tm), lambda i, j: (0, i)),
        ],
        out_shape=[
            jax.ShapeDtypeStruct((m, n), _bf16),
            jax.ShapeDtypeStruct((N_HEADS, m), _f32),
        ],
        scratch_shapes=[pltpu.VMEM((tm, d), _bf16)],
        compiler_params=_params("arbitrary", "arbitrary"),
        name="fox_proj",
    )(x, w, wf_t)


def _cum_kernel(fl_ref, bf_ref, c_ref, *, n_pad):
    x = fl_ref[...] + bf_ref[...]
    lf = jnp.minimum(x, 0.0) - jnp.log1p(jnp.exp(-jnp.abs(x)))
    lane = lax.broadcasted_iota(jnp.int32, x.shape, 1)
    lf = jnp.where(lane >= n_pad, lf, 0.0)
    shift = 1
    while shift < x.shape[1]:
        lf = lf + jnp.where(lane >= shift, pltpu.roll(lf, shift, axis=1), 0.0)
        shift *= 2
    c_ref[...] = lf * LOG2E


def _cum(fl, b_f, *, n_pad):
    b, h, l = fl.shape
    return pl.pallas_call(
        functools.partial(_cum_kernel, n_pad=n_pad),
        grid=(b,),
        in_specs=[
            pl.BlockSpec((None, h, l), lambda i: (i, 0, 0)),
            pl.BlockSpec((h, 1), lambda i: (0, 0)),
        ],
        out_specs=pl.BlockSpec((None, h, l), lambda i: (i, 0, 0)),
        out_shape=jax.ShapeDtypeStruct((b, h, l), _f32),
        compiler_params=_params("arbitrary"),
        name="fox_decay_cumsum",
    )(fl, b_f)


def _lane_replicated_column(row):
    return jnp.broadcast_to(row, (LANES, LANES)).T


def _online_softmax_step(s, v, m_sc, l_sc, acc_sc):
    m_prev = m_sc[...]
    m_new = jnp.maximum(m_prev, jnp.max(s, axis=0, keepdims=True))
    alpha = jnp.exp2(m_prev - m_new)
    p = jnp.exp2(s - m_new)
    l_sc[...] = alpha * l_sc[...] + jnp.sum(p, axis=0, keepdims=True)
    pv = lax.dot_general(v, p.astype(v.dtype), _TN, preferred_element_type=_f32)
    acc_sc[...] = alpha * acc_sc[...] + pv
    m_sc[...] = m_new


def _silu(z):
    return z * jax.nn.sigmoid(z)


def _attn_kernel(q_ref, k_ref, v_ref, z_ref, km_ref, vm_ref, c_ref, cm_ref, o_ref,
                 crep_sc, cmrep_sc, m_sc, l_sc, acc_sc, *, tile):
    qi = pl.program_id(2)
    n_lane_tiles = tile // LANES

    @pl.when(qi == 0)
    def _():
        for ch in range(c_ref.shape[1] // LANES):
            rows = slice(ch * LANES, (ch + 1) * LANES)
            crep_sc[rows, :] = _lane_replicated_column(c_ref[:, rows])
        cmrep_sc[...] = _lane_replicated_column(cm_ref[...])

    q_t = q_ref[...].T
    diag = pl.multiple_of(qi * tile, tile)
    c_shift = jnp.min(c_ref[:, pl.ds(diag, tile)], axis=1, keepdims=True)

    def scores(k, c_rep):
        s = jnp.dot(k, q_t, preferred_element_type=_f32)
        return s + jnp.tile(c_shift - c_rep, (1, n_lane_tiles))

    s = scores(km_ref[...], cmrep_sc[...])
    m = jnp.max(s, axis=0, keepdims=True)
    p = jnp.exp2(s - m)
    m_sc[...] = m
    l_sc[...] = jnp.sum(p, axis=0, keepdims=True)
    acc_sc[...] = lax.dot_general(vm_ref[...], p.astype(_bf16), _TN, preferred_element_type=_f32)

    def below_diagonal(ki, carry):
        rows = pl.ds(pl.multiple_of(ki * tile, tile), tile)
        _online_softmax_step(scores(k_ref[rows, :], crep_sc[rows, :]), v_ref[rows, :],
                             m_sc, l_sc, acc_sc)
        return carry

    lax.fori_loop(0, qi, below_diagonal, 0)

    rows = pl.ds(diag, tile)
    s = scores(k_ref[rows, :], crep_sc[rows, :])
    key = lax.broadcasted_iota(jnp.int32, s.shape, 0)
    qry = lax.broadcasted_iota(jnp.int32, s.shape, 1)
    s = jnp.where(key <= qry, s, MASKED)
    _online_softmax_step(s, v_ref[rows, :], m_sc, l_sc, acc_sc)

    o = (acc_sc[...] * (1.0 / l_sc[...])).T
    o_ref[...] = (o * _silu(z_ref[...].astype(_f32))).astype(o_ref.dtype)


def _attention(qkvz, qkvz_meta, c, c_meta, *, tile):
    b, l, _ = qkvz.shape
    h, dh = N_HEADS, HEAD_DIM
    return pl.pallas_call(
        functools.partial(_attn_kernel, tile=tile),
        grid=(b, h, l // tile),
        in_specs=[
            pl.BlockSpec((None, tile, dh), lambda bi, hi, qi: (bi, qi, hi)),
            pl.BlockSpec((None, l, dh), lambda bi, hi, qi: (bi, 0, h + hi)),
            pl.BlockSpec((None, l, dh), lambda bi, hi, qi: (bi, 0, 2 * h + hi)),
            pl.BlockSpec((None, tile, dh), lambda bi, hi, qi: (bi, qi, 3 * h + hi)),
            pl.BlockSpec((LANES, dh), lambda bi, hi, qi: (0, h + hi)),
            pl.BlockSpec((LANES, dh), lambda bi, hi, qi: (0, 2 * h + hi)),
            pl.BlockSpec((None, None, 1, l), lambda bi, hi, qi: (bi, hi, 0, 0)),
            pl.BlockSpec((None, 1, LANES), lambda bi, hi, qi: (hi, 0, 0)),
        ],
        out_specs=pl.BlockSpec((None, tile, dh), lambda bi, hi, qi: (bi, qi, hi)),
        out_shape=jax.ShapeDtypeStruct((b, l, h * dh), _bf16),
        scratch_shapes=[
            pltpu.VMEM((l, LANES), _f32),
            pltpu.VMEM((LANES, LANES), _f32),
            pltpu.VMEM((1, tile), _f32),
            pltpu.VMEM((1, tile), _f32),
            pltpu.VMEM((dh, tile), _f32),
        ],
        compiler_params=_params("arbitrary", "arbitrary", "arbitrary"),
        name="fox_attention",
    )(qkvz, qkvz, qkvz, qkvz, qkvz_meta, qkvz_meta, c, c_meta)


def _meta_attn_kernel(qkvz_ref, c_ref, o_ref):
    w = N_HEADS * HEAD_DIM
    row = lax.broadcasted_iota(jnp.int32, (N_META, LANES), 0)
    col = lax.broadcasted_iota(jnp.int32, (N_META, LANES), 1)
    for hd in range(N_HEADS):
        lo = hd * HEAD_DIM
        q = qkvz_ref[0:N_META, lo:lo + HEAD_DIM]
        k = qkvz_ref[:, w + lo:w + lo + HEAD_DIM]
        v = qkvz_ref[:, 2 * w + lo:2 * w + lo + HEAD_DIM]
        z = qkvz_ref[0:N_META, 3 * w + lo:3 * w + lo + HEAD_DIM]
        ck = c_ref[hd]
        c_shift = jnp.min(jnp.where(col[0:1] < N_META, ck, NO_KEY), axis=1, keepdims=True)
        s = lax.dot_general(q, k, _NT, preferred_element_type=_f32) + (c_shift - ck)
        s = jnp.where(col <= row, s, MASKED)
        m = jnp.max(s, axis=1, keepdims=True)
        p = jnp.exp2(s - m)
        o = jnp.dot(p.astype(_bf16), v, preferred_element_type=_f32)
        o = o / jnp.sum(p, axis=1, keepdims=True)
        o_ref[:, lo:lo + HEAD_DIM] = (o * _silu(z.astype(_f32))).astype(o_ref.dtype)


def _meta_attention(qkvz_meta, c_meta):
    return pl.pallas_call(
        _meta_attn_kernel,
        out_shape=jax.ShapeDtypeStruct((N_META, N_HEADS * HEAD_DIM), _bf16),
        compiler_params=pltpu.CompilerParams(vmem_limit_bytes=VMEM_LIMIT),
        name="fox_meta_attention",
    )(qkvz_meta, c_meta)


def _layer_norm(r, g, b):
    mu = jnp.mean(r, axis=-1, keepdims=True)
    d = r - mu
    var = jnp.mean(d * d, axis=-1, keepdims=True)
    return d * lax.rsqrt(var + LN_EPS) * g + b


def _out_ln_kernel(a_ref, w_ref, res_ref, g_ref, b_ref, o_ref):
    y = jnp.dot(a_ref[...], w_ref[...], preferred_element_type=_f32)
    o_ref[...] = _layer_norm(ALPHA * res_ref[...] + y, g_ref[...], b_ref[...])


def _resident(shape):
    nd = len(shape)
    return pl.BlockSpec(shape, lambda *_: (0,) * nd, pipeline_mode=pl.Buffered(1))


def _out_ln(a, w, res, g, b, *, tm):
    m, d = res.shape
    k = a.shape[1]
    return pl.pallas_call(
        _out_ln_kernel,
        grid=(m // tm,),
        in_specs=[
            pl.BlockSpec((tm, k), lambda i: (i, 0)),
            _resident((k, d)),
            pl.BlockSpec((tm, d), lambda i: (i, 0)),
            _resident((1, d)),
            _resident((1, d)),
        ],
        out_specs=pl.BlockSpec((tm, d), lambda i: (i, 0)),
        out_shape=jax.ShapeDtypeStruct((m, d), _f32),
        compiler_params=_params("arbitrary"),
        name="out_proj_layernorm",
    )(a, w, res, g, b)


def _matmul_kernel(x_ref, w_ref, o_ref, xb_ref):
    @pl.when(pl.program_id(1) == 0)
    def _():
        xb_ref[...] = x_ref[...].astype(_bf16)

    o_ref[...] = jnp.dot(xb_ref[...], w_ref[...], preferred_element_type=_f32).astype(o_ref.dtype)


def _matmul(x, w, *, col0, n, out_dtype, tm, tn):
    m, d = x.shape
    off = col0 // tn
    return pl.pallas_call(
        _matmul_kernel,
        grid=(m // tm, n // tn),
        in_specs=[
            pl.BlockSpec((tm, d), lambda i, j: (i, 0)),
            pl.BlockSpec((d, tn), lambda i, j: (0, off + j)),
        ],
        out_specs=pl.BlockSpec((tm, tn), lambda i, j: (i, j)),
        out_shape=jax.ShapeDtypeStruct((m, n), out_dtype),
        scratch_shapes=[pltpu.VMEM((tm, d), _bf16)],
        compiler_params=_params("arbitrary", "arbitrary"),
        name="pool_in_proj",
    )(x, w)


def _pool_tail_kernel(u_ref, uprev_ref, umeta_ref, z_ref, h_ref, wg_ref, sc_ref, wo_ref,
                      g_ref, b_ref, o_ref, ext_ref, gate_ref, *, tm, tiles_per_seq):
    halo = max(POOL_WINDOWS)
    first = pl.program_id(0) % tiles_per_seq == 0
    ext_ref[0:halo, :] = jnp.where(first, umeta_ref[...], uprev_ref[...])
    ext_ref[halo:halo + tm, :] = u_ref[...]
    grp = u_ref.shape[1] // len(POOL_WINDOWS)
    for gi, win in enumerate(POOL_WINDOWS):
        cols = slice(gi * grp, (gi + 1) * grp)
        u = ext_ref[halo:halo + tm, cols]
        tot = u
        for back in range(1, win):
            tot = tot + ext_ref[halo - back:halo - back + tm, cols]
        d = tot / float(win) - u
        e = jnp.dot(d.astype(_bf16), wg_ref[gi], preferred_element_type=_f32) * sc_ref[:, cols]
        gate_ref[:, cols] = (e * _silu(z_ref[:, cols].astype(_f32))).astype(_bf16)
    y = jnp.dot(gate_ref[...], wo_ref[...], preferred_element_type=_f32)
    o_ref[...] = _layer_norm(ALPHA * h_ref[...] + y, g_ref[...], b_ref[...])


def _pool_tail(u, u_meta, z, h1, w_grp, scale, w_out, g, b, *, tm, seq):
    m, d = u.shape
    halo = max(POOL_WINDOWS)
    per_halo = tm // halo
    kern = functools.partial(_pool_tail_kernel, tm=tm, tiles_per_seq=seq // tm)
    return pl.pallas_call(
        kern,
        grid=(m // tm,),
        in_specs=[
            pl.BlockSpec((tm, d), lambda i: (i, 0)),
            pl.BlockSpec((halo, d), lambda i: (jnp.maximum(i * per_halo - 1, 0), 0)),
            _resident((halo, d)),
            pl.BlockSpec((tm, d), lambda i: (i, 0)),
            pl.BlockSpec((tm, d), lambda i: (i, 0)),
            _resident(w_grp.shape),
            _resident((1, d)),
            _resident((d, d)),
            _resident((1, d)),
            _resident((1, d)),
        ],
        out_specs=pl.BlockSpec((tm, d), lambda i: (i, 0)),
        out_shape=jax.ShapeDtypeStruct((m, d), _f32),
        scratch_shapes=[pltpu.VMEM((tm + halo, d), _f32), pltpu.VMEM((tm, d), _bf16)],
        compiler_params=_params("arbitrary"),
        name="pool_tail",
    )(u, u, u_meta, z, h1, w_grp, scale, w_out, g, b)


def kernel(x, meta_tokens, fox_w_in, fox_b_f, fox_w_out, ln0_g, ln0_b, pool_w_in, pool_w_grp,
           pool_scale, pool_w_out, ln1_g, ln1_b):
    bsz, seq, d = x.shape
    width = N_HEADS * HEAD_DIM
    m = bsz * seq

    w_in = fox_w_in.astype(_bf16)
    wf_t = fox_w_in[:, 4 * width:].T.astype(_bf16)
    w_out0 = fox_w_out.astype(_bf16)
    w_pin = pool_w_in.astype(_bf16)
    w_grp = pool_w_grp.astype(_bf16)
    w_out1 = pool_w_out.astype(_bf16)
    row = lambda t: t.reshape(1, d).astype(_f32)

    x2 = x.reshape(m, d)
    meta = meta_tokens.astype(_f32)

    qkvz, flt = _fox_proj(x2, w_in, wf_t, tm=1024, tn=1024)
    qkvz_m, flt_m = _fox_proj(meta, w_in, wf_t, tm=N_META, tn=1024)

    n_pad = LANES - N_META
    fl = jnp.concatenate(
        [jnp.zeros((bsz, N_HEADS, n_pad), _f32),
         jnp.broadcast_to(flt_m[None], (bsz, N_HEADS, N_META)),
         flt.reshape(N_HEADS, bsz, seq).transpose(1, 0, 2)], axis=2)
    c_full = _cum(fl, fox_b_f.reshape(N_HEADS, 1).astype(_f32), n_pad=n_pad)
    c_seq = c_full[:, :, LANES:].reshape(bsz, N_HEADS, 1, seq)
    c_meta = jnp.concatenate(
        [c_full[0, :, n_pad:LANES], jnp.full((N_HEADS, n_pad), NO_KEY, _f32)],
        axis=1).reshape(N_HEADS, 1, LANES)
    qkvz_m128 = jnp.pad(qkvz_m, ((0, LANES - N_META), (0, 0)))

    a = _attention(qkvz.reshape(bsz, seq, 4 * width), qkvz_m128, c_seq, c_meta, tile=512)
    a_m = _meta_attention(qkvz_m128, c_meta)

    g0, b0 = row(ln0_g), row(ln0_b)
    h1 = _out_ln(a.reshape(m, width), w_out0, x2, g0, b0, tm=512)
    h1_m = _out_ln(a_m, w_out0, meta, g0, b0, tm=N_META)

    u = _matmul(h1, w_pin, col0=0, n=d, out_dtype=_f32, tm=1024, tn=1024)
    z1 = _matmul(h1, w_pin, col0=d, n=d, out_dtype=_bf16, tm=1024, tn=1024)
    u_m = _matmul(h1_m, w_pin, col0=0, n=d, out_dtype=_f32, tm=N_META, tn=1024)
    out = _pool_tail(u, u_m, z1, h1, w_grp, row(pool_scale), w_out1, row(ln1_g), row(ln1_b),
                     tm=256, seq=seq)
    return out.reshape(bsz, seq, d)
```

```python
import functools
import math

import jax
import jax.numpy as jnp
import numpy as np
from jax import lax
from jax.experimental import pallas as pl
from jax.experimental.pallas import tpu as pltpu

N_META = 16
N_HEADS = 16
HEAD_DIM = 128
POOL_WINDOWS = (2, 4, 8, 16)
DEPTH = 2
ALPHA = (2.0 * DEPTH) ** 0.25
LN_EPS = 1e-5

LANES = 128
MASKED = -1e30
NO_KEY = 1e30
VMEM_LIMIT = 56 * 1024 * 1024
LOG2E = math.log2(math.e)

_NT = (((1,), (1,)), ((), ()))
_TN = (((0,), (0,)), ((), ()))
_f32 = jnp.float32
_bf16 = jnp.bfloat16


def _params(*sem):
    return pltpu.CompilerParams(dimension_semantics=sem, vmem_limit_bytes=VMEM_LIMIT)


def _fox_proj_kernel(x_ref, w_ref, wf_ref, out_ref, flt_ref, xb_ref, *, n_q_tiles, q_scale):
    j = pl.program_id(1)

    @pl.when(j == 0)
    def _():
        xb = x_ref[...].astype(_bf16)
        xb_ref[...] = xb
        flt_ref[...] = lax.dot_general(wf_ref[...], xb, _NT, preferred_element_type=_f32)

    acc = jnp.dot(xb_ref[...], w_ref[...], preferred_element_type=_f32)
    scale = jnp.where(j < n_q_tiles, q_scale, 1.0).astype(_f32)
    out_ref[...] = (acc * scale).astype(out_ref.dtype)


def _fox_proj(x, w, wf_t, *, tm, tn):
    m, d = x.shape
    n = 4 * N_HEADS * HEAD_DIM
    kern = functools.partial(_fox_proj_kernel, n_q_tiles=(N_HEADS * HEAD_DIM) // tn,
                             q_scale=HEAD_DIM ** -0.5 * LOG2E)
    return pl.pallas_call(
        kern,
        grid=(m // tm, n // tn),
        in_specs=[
            pl.BlockSpec((tm, d), lambda i, j: (i, 0)),
            pl.BlockSpec((d, tn), lambda i, j: (0, j)),
            pl.BlockSpec((N_HEADS, d), lambda i, j: (0, 0)),
        ],
        out_specs=[
            pl.BlockSpec((tm, tn), lambda i, j: (i, j)),
            pl.BlockSpec((N_HEADS, tm), lambda i, j: (0, i)),
        ],
        out_shape=[
            jax.ShapeDtypeStruct((m, n), _bf16),
            jax.ShapeDtypeStruct((N_HEADS, m), _f32),
        ],
        scratch_shapes=[pltpu.VMEM((tm, d), _bf16)],
        compiler_params=_params("arbitrary", "arbitrary"),
        name="fox_proj",
    )(x, w, wf_t)


def _cum_kernel(fl_ref, bf_ref, c_ref, *, n_pad):
    x = fl_ref[...] + bf_ref[...]
    lf = jnp.minimum(x, 0.0) - jnp.log1p(jnp.exp(-jnp.abs(x)))
    lane = lax.broadcasted_iota(jnp.int32, x.shape, 1)
    lf = jnp.where(lane >= n_pad, lf, 0.0)
    shift = 1
    while shift < x.shape[1]:
        lf = lf + jnp.where(lane >= shift, pltpu.roll(lf, shift, axis=1), 0.0)
        shift *= 2
    c_ref[...] = lf * LOG2E


def _cum(fl, b_f, *, n_pad):
    b, h, l = fl.shape
    return pl.pallas_call(
        functools.partial(_cum_kernel, n_pad=n_pad),
        grid=(b,),
        in_specs=[
            pl.BlockSpec((None, h, l), lambda i: (i, 0, 0)),
            pl.BlockSpec((h, 1), lambda i: (0, 0)),
        ],
        out_specs=pl.BlockSpec((None, h, l), lambda i: (i, 0, 0)),
        out_shape=jax.ShapeDtypeStruct((b, h, l), _f32),
        compiler_params=_params("arbitrary"),
        name="fox_decay_cumsum",
    )(fl, b_f)


def _lane_replicated_column(row):
    return jnp.broadcast_to(row, (LANES, LANES)).T


def _silu(z):
    return z * jax.nn.sigmoid(z)


def _attention_items(n_q):
    below = [(qi, t) for qi in range(n_q) for t in range(qi)]
    diag = [(qi, qi) for qi in range(n_q)]
    return len(below), np.array(list(zip(*(below + diag))), np.int32)


def _attn_kernel(items_ref, q_ref, k_ref, v_ref, z_ref, km_ref, vm_ref, c_ref, cm_ref, o_ref,
                 crep_sc, cmrep_sc, cshift_sc, qt_sc, m_sc, l_sc, acc_sc,
                 s_even, s_odd, p_even, p_odd, alpha_even, alpha_odd, *, tile, n_below):
    n_q = q_ref.shape[0] // tile
    n_items = n_below + n_q
    n_lane_tiles = tile // LANES
    assert n_below % 2 == 0 and n_q % 2 == 0 and n_below >= 2

    for ch in range(c_ref.shape[1] // LANES):
        rows = slice(ch * LANES, (ch + 1) * LANES)
        crep_sc[rows, :] = _lane_replicated_column(c_ref[:, rows])
    cmrep_sc[...] = _lane_replicated_column(cm_ref[...])

    for qi in range(n_q):
        rows = slice(qi * tile, (qi + 1) * tile)
        qt_sc[qi] = q_ref[rows, :].T
        c_shift = jnp.min(c_ref[:, rows], axis=1, keepdims=True)
        cshift_sc[qi] = jnp.broadcast_to(c_shift, (1, LANES))
        bias = cshift_sc[qi] - cmrep_sc[0:N_META, :]
        s = jnp.dot(km_ref[...], qt_sc[qi], preferred_element_type=_f32)
        s = s + jnp.tile(bias, (1, n_lane_tiles))
        m = jnp.max(s, axis=0, keepdims=True)
        p = jnp.exp2(s - m)
        m_sc[qi] = m
        l_sc[qi] = jnp.sum(p, axis=0, keepdims=True)
        acc_sc[qi] = lax.dot_general(vm_ref[...], p.astype(_bf16), _TN,
                                     preferred_element_type=_f32)

    buffers = ((s_even, p_even, alpha_even), (s_odd, p_odd, alpha_odd))

    def tile_rows(t):
        return pl.ds(pl.multiple_of(t * tile, tile), tile)

    def stage_scores(i, parity, masked):
        qi, rows = items_ref[0, i], tile_rows(items_ref[1, i])
        s = jnp.dot(k_ref[rows, :], qt_sc[qi], preferred_element_type=_f32)
        s = s + jnp.tile(cshift_sc[qi] - crep_sc[rows, :], (1, n_lane_tiles))
        if masked:
            key = lax.broadcasted_iota(jnp.int32, s.shape, 0)
            qry = lax.broadcasted_iota(jnp.int32, s.shape, 1)
            s = jnp.where(key <= qry, s, MASKED)
        buffers[parity][0][...] = s

    def stage_softmax(i, parity):
        s_buf, p_buf, alpha_buf = buffers[parity]
        qi = items_ref[0, i]
        s = s_buf[...]
        m_prev = m_sc[qi]
        m_new = jnp.maximum(m_prev, jnp.max(s, axis=0, keepdims=True))
        alpha = jnp.exp2(m_prev - m_new)
        p = jnp.exp2(s - m_new)
        l_sc[qi] = alpha * l_sc[qi] + jnp.sum(p, axis=0, keepdims=True)
        m_sc[qi] = m_new
        p_buf[...] = p.astype(p_buf.dtype)
        alpha_buf[...] = alpha

    def stage_pv(i, parity):
        _, p_buf, alpha_buf = buffers[parity]
        qi, rows = items_ref[0, i], tile_rows(items_ref[1, i])
        pv = lax.dot_general(v_ref[rows, :], p_buf[...], _TN, preferred_element_type=_f32)
        acc_sc[qi] = alpha_buf[...] * acc_sc[qi] + pv

    def steady_pair(first, masked):
        stage_scores(first, 0, masked)
        stage_softmax(first - 1, 1)
        stage_pv(first - 2, 0)
        stage_scores(first + 1, 1, masked)
        stage_softmax(first, 0)
        stage_pv(first - 1, 1)

    stage_scores(0, 0, False)
    stage_scores(1, 1, False)
    stage_softmax(0, 0)

    def below_pairs(u, carry):
        steady_pair(2 + 2 * u, False)
        return carry

    def diag_pairs(u, carry):
        steady_pair(n_below + 2 * u, True)
        return carry

    lax.fori_loop(0, (n_below - 2) // 2, below_pairs, 0)
    lax.fori_loop(0, n_q // 2, diag_pairs, 0)

    stage_softmax(n_items - 1, 1)
    stage_pv(n_items - 2, 0)
    stage_pv(n_items - 1, 1)

    for qi in range(n_q):
        rows = slice(qi * tile, (qi + 1) * tile)
        o = (acc_sc[qi] * (1.0 / l_sc[qi])).T
        o_ref[rows, :] = (o * _silu(z_ref[rows, :].astype(_f32))).astype(o_ref.dtype)


def _attention(qkvz, qkvz_meta, c, c_meta, *, tile):
    b, l, _ = qkvz.shape
    h, dh = N_HEADS, HEAD_DIM
    n_q = l // tile
    n_below, items = _attention_items(n_q)
    head_cols = lambda first: pl.BlockSpec((None, l, dh), lambda bi, hi, _: (bi, 0, first + hi))
    meta_cols = lambda first: pl.BlockSpec((N_META, dh), lambda bi, hi, _: (0, first + hi))
    grid_spec = pltpu.PrefetchScalarGridSpec(
        num_scalar_prefetch=1,
        grid=(b, h),
        in_specs=[
            head_cols(0), head_cols(h), head_cols(2 * h), head_cols(3 * h),
            meta_cols(h), meta_cols(2 * h),
            pl.BlockSpec((None, None, 1, l), lambda bi, hi, _: (bi, hi, 0, 0)),
            pl.BlockSpec((None, 1, LANES), lambda bi, hi, _: (hi, 0, 0)),
        ],
        out_specs=pl.BlockSpec((None, l, dh), lambda bi, hi, _: (bi, 0, hi)),
        scratch_shapes=[
            pltpu.VMEM((l, LANES), _f32),
            pltpu.VMEM((LANES, LANES), _f32),
            pltpu.VMEM((n_q, 1, LANES), _f32),
            pltpu.VMEM((n_q, dh, tile), _bf16),
            pltpu.VMEM((n_q, 1, tile), _f32),
            pltpu.VMEM((n_q, 1, tile), _f32),
            pltpu.VMEM((n_q, dh, tile), _f32),
            pltpu.VMEM((tile, tile), _f32), pltpu.VMEM((tile, tile), _f32),
            pltpu.VMEM((tile, tile), _bf16), pltpu.VMEM((tile, tile), _bf16),
            pltpu.VMEM((1, tile), _f32), pltpu.VMEM((1, tile), _f32),
        ],
    )
    return pl.pallas_call(
        functools.partial(_attn_kernel, tile=tile, n_below=n_below),
        grid_spec=grid_spec,
        out_shape=jax.ShapeDtypeStruct((b, l, h * dh), _bf16),
        compiler_params=_params("arbitrary", "arbitrary"),
        name="fox_attention",
    )(jnp.asarray(items), qkvz, qkvz, qkvz, qkvz, qkvz_meta, qkvz_meta, c, c_meta)


def _meta_attn_kernel(qkvz_ref, c_ref, o_ref):
    w = N_HEADS * HEAD_DIM
    row = lax.broadcasted_iota(jnp.int32, (N_META, LANES), 0)
    col = lax.broadcasted_iota(jnp.int32, (N_META, LANES), 1)
    for hd in range(N_HEADS):
        lo = hd * HEAD_DIM
        q = qkvz_ref[0:N_META, lo:lo + HEAD_DIM]
        k = qkvz_ref[:, w + lo:w + lo + HEAD_DIM]
        v = qkvz_ref[:, 2 * w + lo:2 * w + lo + HEAD_DIM]
        z = qkvz_ref[0:N_META, 3 * w + lo:3 * w + lo + HEAD_DIM]
        ck = c_ref[hd]
        c_shift = jnp.min(jnp.where(col[0:1] < N_META, ck, NO_KEY), axis=1, keepdims=True)
        s = lax.dot_general(q, k, _NT, preferred_element_type=_f32) + (c_shift - ck)
        s = jnp.where(col <= row, s, MASKED)
        m = jnp.max(s, axis=1, keepdims=True)
        p = jnp.exp2(s - m)
        o = jnp.dot(p.astype(_bf16), v, preferred_element_type=_f32)
        o = o / jnp.sum(p, axis=1, keepdims=True)
        o_ref[:, lo:lo + HEAD_DIM] = (o * _silu(z.astype(_f32))).astype(o_ref.dtype)


def _meta_attention(qkvz_meta, c_meta):
    return pl.pallas_call(
        _meta_attn_kernel,
        out_shape=jax.ShapeDtypeStruct((N_META, N_HEADS * HEAD_DIM), _bf16),
        compiler_params=pltpu.CompilerParams(vmem_limit_bytes=VMEM_LIMIT),
        name="fox_meta_attention",
    )(qkvz_meta, c_meta)


def _layer_norm(r, g, b):
    mu = jnp.mean(r, axis=-1, keepdims=True)
    d = r - mu
    var = jnp.mean(d * d, axis=-1, keepdims=True)
    return d * lax.rsqrt(var + LN_EPS) * g + b


def _out_ln_kernel(a_ref, w_ref, res_ref, g_ref, b_ref, o_ref):
    y = jnp.dot(a_ref[...], w_ref[...], preferred_element_type=_f32)
    o_ref[...] = _layer_norm(ALPHA * res_ref[...] + y, g_ref[...], b_ref[...])


def _resident(shape):
    nd = len(shape)
    return pl.BlockSpec(shape, lambda *_: (0,) * nd, pipeline_mode=pl.Buffered(1))


def _out_ln(a, w, res, g, b, *, tm):
    m, d = res.shape
    k = a.shape[1]
    return pl.pallas_call(
        _out_ln_kernel,
        grid=(m // tm,),
        in_specs=[
            pl.BlockSpec((tm, k), lambda i: (i, 0)),
            _resident((k, d)),
            pl.BlockSpec((tm, d), lambda i: (i, 0)),
            _resident((1, d)),
            _resident((1, d)),
        ],
        out_specs=pl.BlockSpec((tm, d), lambda i: (i, 0)),
        out_shape=jax.ShapeDtypeStruct((m, d), _f32),
        compiler_params=_params("arbitrary"),
        name="out_proj_layernorm",
    )(a, w, res, g, b)


def _matmul_kernel(x_ref, w_ref, o_ref, xb_ref):
    @pl.when(pl.program_id(1) == 0)
    def _():
        xb_ref[...] = x_ref[...].astype(_bf16)

    o_ref[...] = jnp.dot(xb_ref[...], w_ref[...], preferred_element_type=_f32).astype(o_ref.dtype)


def _matmul(x, w, *, col0, n, out_dtype, tm, tn):
    m, d = x.shape
    off = col0 // tn
    return pl.pallas_call(
        _matmul_kernel,
        grid=(m // tm, n // tn),
        in_specs=[
            pl.BlockSpec((tm, d), lambda i, j: (i, 0)),
            pl.BlockSpec((d, tn), lambda i, j: (0, off + j)),
        ],
        out_specs=pl.BlockSpec((tm, tn), lambda i, j: (i, j)),
        out_shape=jax.ShapeDtypeStruct((m, n), out_dtype),
        scratch_shapes=[pltpu.VMEM((tm, d), _bf16)],
        compiler_params=_params("arbitrary", "arbitrary"),
        name="pool_in_proj",
    )(x, w)


def _pool_tail_kernel(u_ref, uprev_ref, umeta_ref, z_ref, h_ref, wg_ref, sc_ref, wo_ref,
                      g_ref, b_ref, o_ref, ext_ref, gate_ref, *, tm, tiles_per_seq):
    halo = max(POOL_WINDOWS)
    first = pl.program_id(0) % tiles_per_seq == 0
    ext_ref[0:halo, :] = jnp.where(first, umeta_ref[...], uprev_ref[...])
    ext_ref[halo:halo + tm, :] = u_ref[...]
    grp = u_ref.shape[1] // len(POOL_WINDOWS)
    for gi, win in enumerate(POOL_WINDOWS):
        cols = slice(gi * grp, (gi + 1) * grp)
        u = ext_ref[halo:halo + tm, cols]
        tot = u
        for back in range(1, win):
            tot = tot + ext_ref[halo - back:halo - back + tm, cols]
        d = tot / float(win) - u
        e = jnp.dot(d.astype(_bf16), wg_ref[gi], preferred_element_type=_f32) * sc_ref[:, cols]
        gate_ref[:, cols] = (e * _silu(z_ref[:, cols].astype(_f32))).astype(_bf16)
    y = jnp.dot(gate_ref[...], wo_ref[...], preferred_element_type=_f32)
    o_ref[...] = _layer_norm(ALPHA * h_ref[...] + y, g_ref[...], b_ref[...])


def _pool_tail(u, u_meta, z, h1, w_grp, scale, w_out, g, b, *, tm, seq):
    m, d = u.shape
    halo = max(POOL_WINDOWS)
    per_halo = tm // halo
    kern = functools.partial(_pool_tail_kernel, tm=tm, tiles_per_seq=seq // tm)
    return pl.pallas_call(
        kern,
        grid=(m // tm,),
        in_specs=[
            pl.BlockSpec((tm, d), lambda i: (i, 0)),
            pl.BlockSpec((halo, d), lambda i: (jnp.maximum(i * per_halo - 1, 0), 0)),
            _resident((halo, d)),
            pl.BlockSpec((tm, d), lambda i: (i, 0)),
            pl.BlockSpec((tm, d), lambda i: (i, 0)),
            _resident(w_grp.shape),
            _resident((1, d)),
            _resident((d, d)),
            _resident((1, d)),
            _resident((1, d)),
        ],
        out_specs=pl.BlockSpec((tm, d), lambda i: (i, 0)),
        out_shape=jax.ShapeDtypeStruct((m, d), _f32),
        scratch_shapes=[pltpu.VMEM((tm + halo, d), _f32), pltpu.VMEM((tm, d), _bf16)],
        compiler_params=_params("arbitrary"),
        name="pool_tail",
    )(u, u, u_meta, z, h1, w_grp, scale, w_out, g, b)


def kernel(x, meta_tokens, fox_w_in, fox_b_f, fox_w_out, ln0_g, ln0_b, pool_w_in, pool_w_grp,
           pool_scale, pool_w_out, ln1_g, ln1_b):
    bsz, seq, d = x.shape
    width = N_HEADS * HEAD_DIM
    m = bsz * seq

    w_in = fox_w_in.astype(_bf16)
    wf_t = fox_w_in[:, 4 * width:].T.astype(_bf16)
    w_out0 = fox_w_out.astype(_bf16)
    w_pin = pool_w_in.astype(_bf16)
    w_grp = pool_w_grp.astype(_bf16)
    w_out1 = pool_w_out.astype(_bf16)
    row = lambda t: t.reshape(1, d).astype(_f32)

    x2 = x.reshape(m, d)
    meta = meta_tokens.astype(_f32)

    qkvz, flt = _fox_proj(x2, w_in, wf_t, tm=1024, tn=1024)
    qkvz_m, flt_m = _fox_proj(meta, w_in, wf_t, tm=N_META, tn=1024)

    n_pad = LANES - N_META
    fl = jnp.concatenate(
        [jnp.zeros((bsz, N_HEADS, n_pad), _f32),
         jnp.broadcast_to(flt_m[None], (bsz, N_HEADS, N_META)),
         flt.reshape(N_HEADS, bsz, seq).transpose(1, 0, 2)], axis=2)
    c_full = _cum(fl, fox_b_f.reshape(N_HEADS, 1).astype(_f32), n_pad=n_pad)
    c_seq = c_full[:, :, LANES:].reshape(bsz, N_HEADS, 1, seq)
    c_meta = jnp.concatenate(
        [c_full[0, :, n_pad:LANES], jnp.full((N_HEADS, n_pad), NO_KEY, _f32)],
        axis=1).reshape(N_HEADS, 1, LANES)
    qkvz_m128 = jnp.pad(qkvz_m, ((0, LANES - N_META), (0, 0)))

    a = _attention(qkvz.reshape(bsz, seq, 4 * width), qkvz_m, c_seq, c_meta, tile=512)
    a_m = _meta_attention(qkvz_m128, c_meta)

    g0, b0 = row(ln0_g), row(ln0_b)
    h1 = _out_ln(a.reshape(m, width), w_out0, x2, g0, b0, tm=512)
    h1_m = _out_ln(a_m, w_out0, meta, g0, b0, tm=N_META)

    u = _matmul(h1, w_pin, col0=0, n=d, out_dtype=_f32, tm=1024, tn=1024)
    z1 = _matmul(h1, w_pin, col0=d, n=d, out_dtype=_bf16, tm=1024, tn=1024)
    u_m = _matmul(h1_m, w_pin, col0=0, n=d, out_dtype=_f32, tm=N_META, tn=1024)
    out = _pool_tail(u, u_m, z1, h1, w_grp, row(pool_scale), w_out1, row(ln1_g), row(ln1_b),
                     tm=256, seq=seq)
    return out.reshape(bsz, seq, d)
```

```python
import functools
import math

import jax
import jax.numpy as jnp
import numpy as np
from jax import lax
from jax.experimental import pallas as pl
from jax.experimental.pallas import tpu as pltpu

N_META = 16
N_HEADS = 16
HEAD_DIM = 128
POOL_WINDOWS = (2, 4, 8, 16)
DEPTH = 2
ALPHA = (2.0 * DEPTH) ** 0.25
LN_EPS = 1e-5

LANES = 128
BF16_SUBLANES = 16
MASKED = -1e30
NO_KEY = 1e30
VMEM_LIMIT = 56 * 1024 * 1024
LOG2E = math.log2(math.e)

_NT = (((1,), (1,)), ((), ()))
_TN = (((0,), (0,)), ((), ()))
_f32 = jnp.float32
_bf16 = jnp.bfloat16


def _params(*sem):
    return pltpu.CompilerParams(dimension_semantics=sem, vmem_limit_bytes=VMEM_LIMIT)


def _fox_proj_kernel(x_ref, w_ref, wf_ref, out_ref, fl_ref, xb_ref, *, n_q_tiles, q_scale):
    j = pl.program_id(1)

    @pl.when(j == 0)
    def _():
        xb = x_ref[...].astype(_bf16)
        xb_ref[...] = xb
        fl_ref[...] = jnp.dot(xb, wf_ref[...], preferred_element_type=_f32)

    acc = jnp.dot(xb_ref[...], w_ref[...], preferred_element_type=_f32)
    scale = jnp.where(j < n_q_tiles, q_scale, 1.0).astype(_f32)
    out_ref[...] = (acc * scale).astype(out_ref.dtype)


def _fox_proj(x, w, wf, *, tm, tn):
    m, d = x.shape
    n = 4 * N_HEADS * HEAD_DIM
    kern = functools.partial(_fox_proj_kernel, n_q_tiles=(N_HEADS * HEAD_DIM) // tn,
                             q_scale=HEAD_DIM ** -0.5 * LOG2E)
    return pl.pallas_call(
        kern,
        grid=(m // tm, n // tn),
        in_specs=[
            pl.BlockSpec((tm, d), lambda i, j: (i, 0)),
            pl.BlockSpec((d, tn), lambda i, j: (0, j)),
            pl.BlockSpec((d, LANES), lambda i, j: (0, 0)),
        ],
        out_specs=[
            pl.BlockSpec((tm, tn), lambda i, j: (i, j)),
            pl.BlockSpec((tm, LANES), lambda i, j: (i, 0)),
        ],
        out_shape=[
            jax.ShapeDtypeStruct((m, n), _bf16),
            jax.ShapeDtypeStruct((m, LANES), _f32),
        ],
        scratch_shapes=[pltpu.VMEM((tm, d), _bf16)],
        compiler_params=_params("arbitrary", "arbitrary"),
        name="fox_proj",
    )(x, w, wf)


def _cum_kernel(fl_ref, bf_ref, c_ref, *, n_pad):
    x = fl_ref[...] + bf_ref[...]
    lf = jnp.minimum(x, 0.0) - jnp.log1p(jnp.exp(-jnp.abs(x)))
    lane = lax.broadcasted_iota(jnp.int32, x.shape, 1)
    lf = jnp.where(lane >= n_pad, lf, 0.0)
    shift = 1
    while shift < x.shape[1]:
        lf = lf + jnp.where(lane >= shift, pltpu.roll(lf, shift, axis=1), 0.0)
        shift *= 2
    c_ref[...] = lf * LOG2E


def _cum(fl, b_f, *, n_pad):
    b, h, l = fl.shape
    return pl.pallas_call(
        functools.partial(_cum_kernel, n_pad=n_pad),
        grid=(b,),
        in_specs=[
            pl.BlockSpec((None, h, l), lambda i: (i, 0, 0)),
            pl.BlockSpec((h, 1), lambda i: (0, 0)),
        ],
        out_specs=pl.BlockSpec((None, h, l), lambda i: (i, 0, 0)),
        out_shape=jax.ShapeDtypeStruct((b, h, l), _f32),
        compiler_params=_params("arbitrary"),
        name="fox_decay_cumsum",
    )(fl, b_f)


def _lane_replicated_column(row):
    return jnp.broadcast_to(row, (LANES, LANES)).T


def _silu(z):
    return z * jax.nn.sigmoid(z)


def _attention_items(n_q):
    below = [(qi, t) for qi in range(n_q) for t in range(qi)]
    diag = [(qi, qi) for qi in range(n_q)]
    return len(below), np.array(list(zip(*(below + diag))), np.int32)


def _attn_kernel(items_ref, q_ref, k_ref, v_ref, z_ref, km_ref, vm_ref, c_ref, cm_ref, o_ref,
                 crep_sc, cmrep_sc, cshift_sc, qt_sc, vt_sc, m_sc, l_sc, acc_sc, *pair_bufs,
                 tile, n_below):
    n_q = q_ref.shape[0] // tile
    dh = q_ref.shape[1]
    n_items = n_below + n_q
    n_lane_tiles = tile // LANES

    for ch in range(c_ref.shape[1] // LANES):
        rows = slice(ch * LANES, (ch + 1) * LANES)
        crep_sc[rows, :] = _lane_replicated_column(c_ref[:, rows])
    cmrep_sc[...] = _lane_replicated_column(cm_ref[...])

    for qi in range(n_q):
        rows = slice(qi * tile, (qi + 1) * tile)
        qt_sc[qi] = q_ref[rows, :].T
        vt_sc[qi, 0:dh, :] = v_ref[rows, :].T
        vt_sc[qi, dh:, :] = jnp.ones((vt_sc.shape[1] - dh, tile), _bf16)
        c_shift = jnp.min(c_ref[:, rows], axis=1, keepdims=True)
        cshift_sc[qi] = jnp.broadcast_to(c_shift, (1, LANES))
        bias = cshift_sc[qi] - cmrep_sc[0:N_META, :]
        s = jnp.dot(km_ref[...], qt_sc[qi], preferred_element_type=_f32)
        s = s + jnp.tile(bias, (1, n_lane_tiles))
        m = jnp.max(s, axis=0, keepdims=True)
        p = jnp.exp2(s - m)
        m_sc[qi] = m
        l_sc[qi] = jnp.sum(p, axis=0, keepdims=True)
        acc_sc[qi] = lax.dot_general(vm_ref[...], p.astype(_bf16), _TN,
                                     preferred_element_type=_f32)

    s_bufs, p_bufs, alpha_bufs, smax_bufs = (pair_bufs[4 * n:4 * n + 4] for n in range(4))

    def tile_rows(t):
        return pl.ds(pl.multiple_of(t * tile, tile), tile)

    def stage_scores(i, buf, masked):
        qi, rows = items_ref[0, i], tile_rows(items_ref[1, i])
        s = jnp.dot(k_ref[rows, :], qt_sc[qi], preferred_element_type=_f32)
        s = s + jnp.tile(cshift_sc[qi] - crep_sc[rows, :], (1, n_lane_tiles))
        if masked:
            key = lax.broadcasted_iota(jnp.int32, s.shape, 0)
            qry = lax.broadcasted_iota(jnp.int32, s.shape, 1)
            s = jnp.where(key <= qry, s, MASKED)
        s_bufs[buf][...] = s
        smax_bufs[buf][...] = jnp.max(s, axis=0, keepdims=True)

    def stage_softmax(i, buf):
        qi = items_ref[0, i]
        s = s_bufs[buf][...]
        m_prev = m_sc[qi]
        m_new = jnp.maximum(m_prev, smax_bufs[buf][...])
        alpha = jnp.exp2(m_prev - m_new)
        p = jnp.exp2(s - m_new)
        m_sc[qi] = m_new
        p_bufs[buf][...] = p.astype(_bf16)
        alpha_bufs[buf][...] = alpha

    def stage_pv(i, buf):
        qi, t = items_ref[0, i], items_ref[1, i]
        alpha = alpha_bufs[buf][...]
        pv = jnp.dot(vt_sc[t], p_bufs[buf][...], preferred_element_type=_f32)
        acc_sc[qi] = alpha * acc_sc[qi] + pv[0:dh]
        l_sc[qi] = alpha * l_sc[qi] + pv[dh:dh + 1]

    def block(j, parity, *, scores=None, softmax=True, pv=True):
        for slot in range(2):
            if scores is not None:
                stage_scores(2 * j + slot, 2 * parity + slot, scores)
            if softmax:
                stage_softmax(2 * (j - 1) + slot, 2 * (1 - parity) + slot)
            if pv:
                stage_pv(2 * (j - 2) + slot, 2 * parity + slot)

    n_pairs = n_items // 2
    first_diag_pair = n_below // 2
    block(0, 0, scores=False, softmax=False, pv=False)
    block(1, 1, scores=False, pv=False)

    def steady(masked, first_block):
        def body(u, carry):
            block(first_block + 2 * u, 0, scores=masked)
            block(first_block + 2 * u + 1, 1, scores=masked)
            return carry
        return body

    lax.fori_loop(0, (first_diag_pair - 2) // 2, steady(False, 2), 0)
    lax.fori_loop(0, (n_pairs - first_diag_pair) // 2, steady(True, first_diag_pair), 0)
    block(n_pairs, 0)
    block(n_pairs + 1, 1, softmax=False)

    for qi in range(n_q):
        rows = slice(qi * tile, (qi + 1) * tile)
        o = (acc_sc[qi] * (1.0 / l_sc[qi])).T
        o_ref[rows, :] = (o * _silu(z_ref[rows, :].astype(_f32))).astype(o_ref.dtype)


def _attention(qkvz, qkvz_meta, c, c_meta, *, tile):
    b, l, _ = qkvz.shape
    h, dh = N_HEADS, HEAD_DIM
    n_q = l // tile
    n_below, items = _attention_items(n_q)
    assert n_below % 4 == 0 and n_q % 4 == 0 and n_below >= 8
    head_cols = lambda first: pl.BlockSpec((None, l, dh), lambda bi, hi, _: (bi, 0, first + hi))
    meta_cols = lambda first: pl.BlockSpec((N_META, dh), lambda bi, hi, _: (0, first + hi))
    grid_spec = pltpu.PrefetchScalarGridSpec(
        num_scalar_prefetch=1,
        grid=(b, h),
        in_specs=[
            head_cols(0), head_cols(h), head_cols(2 * h), head_cols(3 * h),
            meta_cols(h), meta_cols(2 * h),
            pl.BlockSpec((None, None, 1, l), lambda bi, hi, _: (bi, hi, 0, 0)),
            pl.BlockSpec((None, 1, LANES), lambda bi, hi, _: (hi, 0, 0)),
        ],
        out_specs=pl.BlockSpec((None, l, dh), lambda bi, hi, _: (bi, 0, hi)),
        scratch_shapes=[
            pltpu.VMEM((l, LANES), _f32),
            pltpu.VMEM((LANES, LANES), _f32),
            pltpu.VMEM((n_q, 1, LANES), _f32),
            pltpu.VMEM((n_q, dh, tile), _bf16),
            pltpu.VMEM((n_q, dh + BF16_SUBLANES, tile), _bf16),
            pltpu.VMEM((n_q, 1, tile), _f32),
            pltpu.VMEM((n_q, 1, tile), _f32),
            pltpu.VMEM((n_q, dh, tile), _f32),
        ] + [pltpu.VMEM((tile, tile), _f32)] * 4
          + [pltpu.VMEM((tile, tile), _bf16)] * 4
          + [pltpu.VMEM((1, tile), _f32)] * 4
          + [pltpu.VMEM((1, tile), _f32)] * 4,
    )
    return pl.pallas_call(
        functools.partial(_attn_kernel, tile=tile, n_below=n_below),
        grid_spec=grid_spec,
        out_shape=jax.ShapeDtypeStruct((b, l, h * dh), _bf16),
        compiler_params=_params("arbitrary", "arbitrary"),
        name="fox_attention",
    )(jnp.asarray(items), qkvz, qkvz, qkvz, qkvz, qkvz_meta, qkvz_meta, c, c_meta)


def _meta_attn_kernel(qkvz_ref, c_ref, o_ref):
    w = N_HEADS * HEAD_DIM
    row = lax.broadcasted_iota(jnp.int32, (N_META, LANES), 0)
    col = lax.broadcasted_iota(jnp.int32, (N_META, LANES), 1)
    for hd in range(N_HEADS):
        lo = hd * HEAD_DIM
        q = qkvz_ref[0:N_META, lo:lo + HEAD_DIM]
        k = qkvz_ref[:, w + lo:w + lo + HEAD_DIM]
        v = qkvz_ref[:, 2 * w + lo:2 * w + lo + HEAD_DIM]
        z = qkvz_ref[0:N_META, 3 * w + lo:3 * w + lo + HEAD_DIM]
        ck = c_ref[hd]
        c_shift = jnp.min(jnp.where(col[0:1] < N_META, ck, NO_KEY), axis=1, keepdims=True)
        s = lax.dot_general(q, k, _NT, preferred_element_type=_f32) + (c_shift - ck)
        s = jnp.where(col <= row, s, MASKED)
        m = jnp.max(s, axis=1, keepdims=True)
        p = jnp.exp2(s - m)
        o = jnp.dot(p.astype(_bf16), v, preferred_element_type=_f32)
        o = o / jnp.sum(p, axis=1, keepdims=True)
        o_ref[:, lo:lo + HEAD_DIM] = (o * _silu(z.astype(_f32))).astype(o_ref.dtype)


def _meta_attention(qkvz_meta, c_meta):
    return pl.pallas_call(
        _meta_attn_kernel,
        out_shape=jax.ShapeDtypeStruct((N_META, N_HEADS * HEAD_DIM), _bf16),
        compiler_params=pltpu.CompilerParams(vmem_limit_bytes=VMEM_LIMIT),
        name="fox_meta_attention",
    )(qkvz_meta, c_meta)


def _layer_norm(r, g, b):
    mu = jnp.mean(r, axis=-1, keepdims=True)
    d = r - mu
    var = jnp.mean(d * d, axis=-1, keepdims=True)
    return d * lax.rsqrt(var + LN_EPS) * g + b


def _out_ln_kernel(a_ref, w_ref, res_ref, g_ref, b_ref, o_ref):
    y = jnp.dot(a_ref[...], w_ref[...], preferred_element_type=_f32)
    o_ref[...] = _layer_norm(ALPHA * res_ref[...] + y, g_ref[...], b_ref[...])


def _resident(shape):
    nd = len(shape)
    return pl.BlockSpec(shape, lambda *_: (0,) * nd, pipeline_mode=pl.Buffered(1))


def _out_ln(a, w, res, g, b, *, tm):
    m, d = res.shape
    k = a.shape[1]
    return pl.pallas_call(
        _out_ln_kernel,
        grid=(m // tm,),
        in_specs=[
            pl.BlockSpec((tm, k), lambda i: (i, 0)),
            _resident((k, d)),
            pl.BlockSpec((tm, d), lambda i: (i, 0)),
            _resident((1, d)),
            _resident((1, d)),
        ],
        out_specs=pl.BlockSpec((tm, d), lambda i: (i, 0)),
        out_shape=jax.ShapeDtypeStruct((m, d), _f32),
        compiler_params=_params("arbitrary"),
        name="out_proj_layernorm",
    )(a, w, res, g, b)


def _matmul_kernel(x_ref, w_ref, o_ref, xb_ref):
    @pl.when(pl.program_id(1) == 0)
    def _():
        xb_ref[...] = x_ref[...].astype(_bf16)

    o_ref[...] = jnp.dot(xb_ref[...], w_ref[...], preferred_element_type=_f32).astype(o_ref.dtype)


def _matmul(x, w, *, col0, n, out_dtype, tm, tn):
    m, d = x.shape
    off = col0 // tn
    return pl.pallas_call(
        _matmul_kernel,
        grid=(m // tm, n // tn),
        in_specs=[
            pl.BlockSpec((tm, d), lambda i, j: (i, 0)),
            pl.BlockSpec((d, tn), lambda i, j: (0, off + j)),
        ],
        out_specs=pl.BlockSpec((tm, tn), lambda i, j: (i, j)),
        out_shape=jax.ShapeDtypeStruct((m, n), out_dtype),
        scratch_shapes=[pltpu.VMEM((tm, d), _bf16)],
        compiler_params=_params("arbitrary", "arbitrary"),
        name="pool_in_proj",
    )(x, w)


def _pool_tail_kernel(u_ref, uprev_ref, umeta_ref, z_ref, h_ref, wg_ref, sc_ref, wo_ref,
                      g_ref, b_ref, o_ref, ext_ref, gate_ref, *, tm, tiles_per_seq):
    halo = max(POOL_WINDOWS)
    first = pl.program_id(0) % tiles_per_seq == 0
    ext_ref[0:halo, :] = jnp.where(first, umeta_ref[...], uprev_ref[...])
    ext_ref[halo:halo + tm, :] = u_ref[...]
    grp = u_ref.shape[1] // len(POOL_WINDOWS)
    for gi, win in enumerate(POOL_WINDOWS):
        cols = slice(gi * grp, (gi + 1) * grp)
        u = ext_ref[halo:halo + tm, cols]
        tot = u
        for back in range(1, win):
            tot = tot + ext_ref[halo - back:halo - back + tm, cols]
        d = tot / float(win) - u
        e = jnp.dot(d.astype(_bf16), wg_ref[gi], preferred_element_type=_f32) * sc_ref[:, cols]
        gate_ref[:, cols] = (e * _silu(z_ref[:, cols].astype(_f32))).astype(_bf16)
    y = jnp.dot(gate_ref[...], wo_ref[...], preferred_element_type=_f32)
    o_ref[...] = _layer_norm(ALPHA * h_ref[...] + y, g_ref[...], b_ref[...])


def _pool_tail(u, u_meta, z, h1, w_grp, scale, w_out, g, b, *, tm, seq):
    m, d = u.shape
    halo = max(POOL_WINDOWS)
    per_halo = tm // halo
    kern = functools.partial(_pool_tail_kernel, tm=tm, tiles_per_seq=seq // tm)
    return pl.pallas_call(
        kern,
        grid=(m // tm,),
        in_specs=[
            pl.BlockSpec((tm, d), lambda i: (i, 0)),
            pl.BlockSpec((halo, d), lambda i: (jnp.maximum(i * per_halo - 1, 0), 0)),
            _resident((halo, d)),
            pl.BlockSpec((tm, d), lambda i: (i, 0)),
            pl.BlockSpec((tm, d), lambda i: (i, 0)),
            _resident(w_grp.shape),
            _resident((1, d)),
            _resident((d, d)),
            _resident((1, d)),
            _resident((1, d)),
        ],
        out_specs=pl.BlockSpec((tm, d), lambda i: (i, 0)),
        out_shape=jax.ShapeDtypeStruct((m, d), _f32),
        scratch_shapes=[pltpu.VMEM((tm + halo, d), _f32), pltpu.VMEM((tm, d), _bf16)],
        compiler_params=_params("arbitrary"),
        name="pool_tail",
    )(u, u, u_meta, z, h1, w_grp, scale, w_out, g, b)


def kernel(x, meta_tokens, fox_w_in, fox_b_f, fox_w_out, ln0_g, ln0_b, pool_w_in, pool_w_grp,
           pool_scale, pool_w_out, ln1_g, ln1_b):
    bsz, seq, d = x.shape
    width = N_HEADS * HEAD_DIM
    m = bsz * seq

    w_in = fox_w_in[:, :4 * width].astype(_bf16)
    wf = jnp.pad(fox_w_in[:, 4 * width:], ((0, 0), (0, LANES - N_HEADS))).astype(_bf16)
    w_out0 = fox_w_out.astype(_bf16)
    w_pin = pool_w_in.astype(_bf16)
    w_grp = pool_w_grp.astype(_bf16)
    w_out1 = pool_w_out.astype(_bf16)
    row = lambda t: t.reshape(1, d).astype(_f32)

    x2 = x.reshape(m, d)
    meta = meta_tokens.astype(_f32)

    qkvz, fl_seq = _fox_proj(x2, w_in, wf, tm=1024, tn=1024)
    qkvz_m, fl_meta = _fox_proj(meta, w_in, wf, tm=N_META, tn=1024)

    n_pad = LANES - N_META
    fl = jnp.concatenate(
        [jnp.zeros((bsz, N_HEADS, n_pad), _f32),
         jnp.broadcast_to(fl_meta[:, :N_HEADS].T[None], (bsz, N_HEADS, N_META)),
         fl_seq[:, :N_HEADS].reshape(bsz, seq, N_HEADS).transpose(0, 2, 1)], axis=2)
    c_full = _cum(fl, fox_b_f.reshape(N_HEADS, 1).astype(_f32), n_pad=n_pad)
    c_seq = c_full[:, :, LANES:].reshape(bsz, N_HEADS, 1, seq)
    c_meta = jnp.concatenate(
        [c_full[0, :, n_pad:LANES], jnp.full((N_HEADS, n_pad), NO_KEY, _f32)],
        axis=1).reshape(N_HEADS, 1, LANES)
    qkvz_m128 = jnp.pad(qkvz_m, ((0, LANES - N_META), (0, 0)))

    a = _attention(qkvz.reshape(bsz, seq, 4 * width), qkvz_m, c_seq, c_meta, tile=512)
    a_m = _meta_attention(qkvz_m128, c_meta)

    g0, b0 = row(ln0_g), row(ln0_b)
    h1 = _out_ln(a.reshape(m, width), w_out0, x2, g0, b0, tm=512)
    h1_m = _out_ln(a_m, w_out0, meta, g0, b0, tm=N_META)

    u = _matmul(h1, w_pin, col0=0, n=d, out_dtype=_f32, tm=1024, tn=1024)
    z1 = _matmul(h1, w_pin, col0=d, n=d, out_dtype=_bf16, tm=1024, tn=1024)
    u_m = _matmul(h1_m, w_pin, col0=0, n=d, out_dtype=_f32, tm=N_META, tn=1024)
    out = _pool_tail(u, u_m, z1, h1, w_grp, row(pool_scale), w_out1, row(ln1_g), row(ln1_b),
                     tm=256, seq=seq)
    return out.reshape(bsz, seq, d)
```

```python
import functools
import math

import jax
import jax.numpy as jnp
import numpy as np
from jax import lax
from jax.experimental import pallas as pl
from jax.experimental.pallas import tpu as pltpu

N_META = 16
N_HEADS = 16
HEAD_DIM = 128
POOL_WINDOWS = (2, 4, 8, 16)
DEPTH = 2
ALPHA = (2.0 * DEPTH) ** 0.25
LN_EPS = 1e-5

LANES = 128
BF16_SUBLANES = 16
MASKED = -1e30
NO_KEY = 1e30
VMEM_LIMIT = 56 * 1024 * 1024
LOG2E = math.log2(math.e)

_NT = (((1,), (1,)), ((), ()))
_TN = (((0,), (0,)), ((), ()))
_f32 = jnp.float32
_bf16 = jnp.bfloat16


def _params(*sem):
    return pltpu.CompilerParams(dimension_semantics=sem, vmem_limit_bytes=VMEM_LIMIT)


def _fox_proj_kernel(x_ref, w_ref, wf_ref, out_ref, fl_ref, xb_ref, *, n_q_tiles, q_scale):
    j = pl.program_id(1)

    @pl.when(j == 0)
    def _():
        xb = x_ref[...].astype(_bf16)
        xb_ref[...] = xb
        fl_ref[...] = lax.dot_general(xb, wf_ref[...], _NT, preferred_element_type=_f32)

    acc = lax.dot_general(xb_ref[...], w_ref[...], _NT, preferred_element_type=_f32)
    scale = jnp.where(j < n_q_tiles, q_scale, 1.0).astype(_f32)
    out_ref[...] = (acc * scale).astype(out_ref.dtype)


def _fox_proj(x, w, wf, *, tm, tn):
    m, d = x.shape
    n = 4 * N_HEADS * HEAD_DIM
    kern = functools.partial(_fox_proj_kernel, n_q_tiles=(N_HEADS * HEAD_DIM) // tn,
                             q_scale=HEAD_DIM ** -0.5 * LOG2E)
    return pl.pallas_call(
        kern,
        grid=(m // tm, n // tn),
        in_specs=[
            pl.BlockSpec((tm, d), lambda i, j: (i, 0)),
            pl.BlockSpec((tn, d), lambda i, j: (j, 0)),
            pl.BlockSpec((LANES, d), lambda i, j: (0, 0)),
        ],
        out_specs=[
            pl.BlockSpec((tm, tn), lambda i, j: (i, j)),
            pl.BlockSpec((tm, LANES), lambda i, j: (i, 0)),
        ],
        out_shape=[
            jax.ShapeDtypeStruct((m, n), _bf16),
            jax.ShapeDtypeStruct((m, LANES), _f32),
        ],
        scratch_shapes=[pltpu.VMEM((tm, d), _bf16)],
        compiler_params=_params("arbitrary", "arbitrary"),
        name="fox_proj",
    )(x, w, wf)


def _cum_kernel(fl_ref, bf_ref, c_ref, *, n_pad):
    x = fl_ref[...] + bf_ref[...]
    lf = jnp.minimum(x, 0.0) - jnp.log1p(jnp.exp(-jnp.abs(x)))
    lane = lax.broadcasted_iota(jnp.int32, x.shape, 1)
    lf = jnp.where(lane >= n_pad, lf, 0.0)
    shift = 1
    while shift < x.shape[1]:
        lf = lf + jnp.where(lane >= shift, pltpu.roll(lf, shift, axis=1), 0.0)
        shift *= 2
    c_ref[...] = lf * LOG2E


def _cum(fl, b_f, *, n_pad):
    b, h, l = fl.shape
    return pl.pallas_call(
        functools.partial(_cum_kernel, n_pad=n_pad),
        grid=(b,),
        in_specs=[
            pl.BlockSpec((None, h, l), lambda i: (i, 0, 0)),
            pl.BlockSpec((h, 1), lambda i: (0, 0)),
        ],
        out_specs=pl.BlockSpec((None, h, l), lambda i: (i, 0, 0)),
        out_shape=jax.ShapeDtypeStruct((b, h, l), _f32),
        compiler_params=_params("arbitrary"),
        name="fox_decay_cumsum",
    )(fl, b_f)


def _lane_replicated_column(row):
    return jnp.broadcast_to(row, (LANES, LANES)).T


def _silu(z):
    return z * jax.nn.sigmoid(z)


def _attention_items(n_q):
    below = [(qi, t) for qi in range(n_q) for t in range(qi)]
    diag = [(qi, qi) for qi in range(n_q)]
    return len(below), np.array(list(zip(*(below + diag))), np.int32)


def _attn_kernel(items_ref, q_ref, k_ref, v_ref, z_ref, km_ref, vm_ref, c_ref, cm_ref, o_ref,
                 crep_sc, cmrep_sc, cshift_sc, qt_sc, vt_sc, m_sc, l_sc, acc_sc, *pair_bufs,
                 tile, n_below):
    n_q = q_ref.shape[0] // tile
    dh = q_ref.shape[1]
    n_items = n_below + n_q
    n_lane_tiles = tile // LANES

    for ch in range(c_ref.shape[1] // LANES):
        rows = slice(ch * LANES, (ch + 1) * LANES)
        crep_sc[rows, :] = _lane_replicated_column(c_ref[:, rows])
    cmrep_sc[...] = _lane_replicated_column(cm_ref[...])

    for qi in range(n_q):
        rows = slice(qi * tile, (qi + 1) * tile)
        qt_sc[qi] = q_ref[rows, :].T
        vt_sc[qi, 0:dh, :] = v_ref[rows, :].T
        vt_sc[qi, dh:, :] = jnp.ones((vt_sc.shape[1] - dh, tile), _bf16)
        c_shift = jnp.min(c_ref[:, rows], axis=1, keepdims=True)
        cshift_sc[qi] = jnp.broadcast_to(c_shift, (1, LANES))
        bias = cshift_sc[qi] - cmrep_sc[0:N_META, :]
        s = jnp.dot(km_ref[...], qt_sc[qi], preferred_element_type=_f32)
        s = s + jnp.tile(bias, (1, n_lane_tiles))
        m = jnp.max(s, axis=0, keepdims=True)
        p = jnp.exp2(s - m)
        m_sc[qi] = m
        l_sc[qi] = jnp.sum(p, axis=0, keepdims=True)
        acc_sc[qi] = lax.dot_general(vm_ref[...], p.astype(_bf16), _TN,
                                     preferred_element_type=_f32)

    s_bufs, p_bufs, alpha_bufs, smax_bufs = (pair_bufs[4 * n:4 * n + 4] for n in range(4))

    def tile_rows(t):
        return pl.ds(pl.multiple_of(t * tile, tile), tile)

    def stage_scores(i, buf, masked):
        qi, rows = items_ref[0, i], tile_rows(items_ref[1, i])
        s = jnp.dot(k_ref[rows, :], qt_sc[qi], preferred_element_type=_f32)
        s = s + jnp.tile(cshift_sc[qi] - crep_sc[rows, :], (1, n_lane_tiles))
        if masked:
            key = lax.broadcasted_iota(jnp.int32, s.shape, 0)
            qry = lax.broadcasted_iota(jnp.int32, s.shape, 1)
            s = jnp.where(key <= qry, s, MASKED)
        s_bufs[buf][...] = s
        smax_bufs[buf][...] = jnp.max(s, axis=0, keepdims=True)

    def stage_softmax(i, buf):
        qi = items_ref[0, i]
        s = s_bufs[buf][...]
        m_prev = m_sc[qi]
        m_new = jnp.maximum(m_prev, smax_bufs[buf][...])
        alpha = jnp.exp2(m_prev - m_new)
        p = jnp.exp2(s - m_new)
        m_sc[qi] = m_new
        p_bufs[buf][...] = p.astype(_bf16)
        alpha_bufs[buf][...] = alpha

    def stage_pv(i, buf):
        qi, t = items_ref[0, i], items_ref[1, i]
        alpha = alpha_bufs[buf][...]
        pv = jnp.dot(vt_sc[t], p_bufs[buf][...], preferred_element_type=_f32)
        acc_sc[qi] = alpha * acc_sc[qi] + pv[0:dh]
        l_sc[qi] = alpha * l_sc[qi] + pv[dh:dh + 1]

    def block(j, parity, *, scores=None, softmax=True, pv=True):
        for slot in range(2):
            if scores is not None:
                stage_scores(2 * j + slot, 2 * parity + slot, scores)
            if softmax:
                stage_softmax(2 * (j - 1) + slot, 2 * (1 - parity) + slot)
            if pv:
                stage_pv(2 * (j - 2) + slot, 2 * parity + slot)

    n_pairs = n_items // 2
    first_diag_pair = n_below // 2
    block(0, 0, scores=False, softmax=False, pv=False)
    block(1, 1, scores=False, pv=False)

    def steady(masked, first_block):
        def body(u, carry):
            block(first_block + 2 * u, 0, scores=masked)
            block(first_block + 2 * u + 1, 1, scores=masked)
            return carry
        return body

    lax.fori_loop(0, (first_diag_pair - 2) // 2, steady(False, 2), 0)
    lax.fori_loop(0, (n_pairs - first_diag_pair) // 2, steady(True, first_diag_pair), 0)
    block(n_pairs, 0)
    block(n_pairs + 1, 1, softmax=False)

    for qi in range(n_q):
        rows = slice(qi * tile, (qi + 1) * tile)
        o = (acc_sc[qi] * (1.0 / l_sc[qi])).T
        o_ref[rows, :] = (o * _silu(z_ref[rows, :].astype(_f32))).astype(o_ref.dtype)


def _attention(qkvz, qkvz_meta, c, c_meta, *, tile):
    b, l, _ = qkvz.shape
    h, dh = N_HEADS, HEAD_DIM
    n_q = l // tile
    n_below, items = _attention_items(n_q)
    assert n_below % 4 == 0 and n_q % 4 == 0 and n_below >= 8
    head_cols = lambda first: pl.BlockSpec((None, l, dh), lambda bi, hi, _: (bi, 0, first + hi))
    meta_cols = lambda first: pl.BlockSpec((N_META, dh), lambda bi, hi, _: (0, first + hi))
    grid_spec = pltpu.PrefetchScalarGridSpec(
        num_scalar_prefetch=1,
        grid=(b, h),
        in_specs=[
            head_cols(0), head_cols(h), head_cols(2 * h), head_cols(3 * h),
            meta_cols(h), meta_cols(2 * h),
            pl.BlockSpec((None, None, 1, l), lambda bi, hi, _: (bi, hi, 0, 0)),
            pl.BlockSpec((None, 1, LANES), lambda bi, hi, _: (hi, 0, 0)),
        ],
        out_specs=pl.BlockSpec((None, l, dh), lambda bi, hi, _: (bi, 0, hi)),
        scratch_shapes=[
            pltpu.VMEM((l, LANES), _f32),
            pltpu.VMEM((LANES, LANES), _f32),
            pltpu.VMEM((n_q, 1, LANES), _f32),
            pltpu.VMEM((n_q, dh, tile), _bf16),
            pltpu.VMEM((n_q, dh + BF16_SUBLANES, tile), _bf16),
            pltpu.VMEM((n_q, 1, tile), _f32),
            pltpu.VMEM((n_q, 1, tile), _f32),
            pltpu.VMEM((n_q, dh, tile), _f32),
        ] + [pltpu.VMEM((tile, tile), _f32)] * 4
          + [pltpu.VMEM((tile, tile), _bf16)] * 4
          + [pltpu.VMEM((1, tile), _f32)] * 4
          + [pltpu.VMEM((1, tile), _f32)] * 4,
    )
    return pl.pallas_call(
        functools.partial(_attn_kernel, tile=tile, n_below=n_below),
        grid_spec=grid_spec,
        out_shape=jax.ShapeDtypeStruct((b, l, h * dh), _bf16),
        compiler_params=_params("arbitrary", "arbitrary"),
        name="fox_attention",
    )(jnp.asarray(items), qkvz, qkvz, qkvz, qkvz, qkvz_meta, qkvz_meta, c, c_meta)


def _meta_attn_kernel(qkvz_ref, c_ref, o_ref):
    w = N_HEADS * HEAD_DIM
    row = lax.broadcasted_iota(jnp.int32, (N_META, LANES), 0)
    col = lax.broadcasted_iota(jnp.int32, (N_META, LANES), 1)
    for hd in range(N_HEADS):
        lo = hd * HEAD_DIM
        q = qkvz_ref[0:N_META, lo:lo + HEAD_DIM]
        k = qkvz_ref[:, w + lo:w + lo + HEAD_DIM]
        v = qkvz_ref[:, 2 * w + lo:2 * w + lo + HEAD_DIM]
        z = qkvz_ref[0:N_META, 3 * w + lo:3 * w + lo + HEAD_DIM]
        ck = c_ref[hd]
        c_shift = jnp.min(jnp.where(col[0:1] < N_META, ck, NO_KEY), axis=1, keepdims=True)
        s = lax.dot_general(q, k, _NT, preferred_element_type=_f32) + (c_shift - ck)
        s = jnp.where(col <= row, s, MASKED)
        m = jnp.max(s, axis=1, keepdims=True)
        p = jnp.exp2(s - m)
        o = jnp.dot(p.astype(_bf16), v, preferred_element_type=_f32)
        o = o / jnp.sum(p, axis=1, keepdims=True)
        o_ref[:, lo:lo + HEAD_DIM] = (o * _silu(z.astype(_f32))).astype(o_ref.dtype)


def _meta_attention(qkvz_meta, c_meta):
    return pl.pallas_call(
        _meta_attn_kernel,
        out_shape=jax.ShapeDtypeStruct((N_META, N_HEADS * HEAD_DIM), _bf16),
        compiler_params=pltpu.CompilerParams(vmem_limit_bytes=VMEM_LIMIT),
        name="fox_meta_attention",
    )(qkvz_meta, c_meta)


def _layer_norm(r, g, b):
    mu = jnp.mean(r, axis=-1, keepdims=True)
    d = r - mu
    var = jnp.mean(d * d, axis=-1, keepdims=True)
    return d * lax.rsqrt(var + LN_EPS) * g + b


def _out_ln_kernel(a_ref, w_ref, res_ref, g_ref, b_ref, o_ref):
    y = jnp.dot(a_ref[...], w_ref[...], preferred_element_type=_f32)
    o_ref[...] = _layer_norm(ALPHA * res_ref[...] + y, g_ref[...], b_ref[...])


def _resident(shape):
    nd = len(shape)
    return pl.BlockSpec(shape, lambda *_: (0,) * nd, pipeline_mode=pl.Buffered(1))


def _out_ln(a, w, res, g, b, *, tm):
    m, d = res.shape
    k = a.shape[1]
    return pl.pallas_call(
        _out_ln_kernel,
        grid=(m // tm,),
        in_specs=[
            pl.BlockSpec((tm, k), lambda i: (i, 0)),
            _resident((k, d)),
            pl.BlockSpec((tm, d), lambda i: (i, 0)),
            _resident((1, d)),
            _resident((1, d)),
        ],
        out_specs=pl.BlockSpec((tm, d), lambda i: (i, 0)),
        out_shape=jax.ShapeDtypeStruct((m, d), _f32),
        compiler_params=_params("arbitrary"),
        name="out_proj_layernorm",
    )(a, w, res, g, b)


def _matmul_kernel(x_ref, w_ref, o_ref, xb_ref):
    @pl.when(pl.program_id(1) == 0)
    def _():
        xb_ref[...] = x_ref[...].astype(_bf16)

    o_ref[...] = jnp.dot(xb_ref[...], w_ref[...], preferred_element_type=_f32).astype(o_ref.dtype)


def _matmul(x, w, *, col0, n, out_dtype, tm, tn):
    m, d = x.shape
    off = col0 // tn
    return pl.pallas_call(
        _matmul_kernel,
        grid=(m // tm, n // tn),
        in_specs=[
            pl.BlockSpec((tm, d), lambda i, j: (i, 0)),
            pl.BlockSpec((d, tn), lambda i, j: (0, off + j)),
        ],
        out_specs=pl.BlockSpec((tm, tn), lambda i, j: (i, j)),
        out_shape=jax.ShapeDtypeStruct((m, n), out_dtype),
        scratch_shapes=[pltpu.VMEM((tm, d), _bf16)],
        compiler_params=_params("arbitrary", "arbitrary"),
        name="pool_in_proj",
    )(x, w)


def _layer_boundary_kernel(a_ref, wo_ref, res_ref, g_ref, b_ref, wi_ref, umeta_ref,
                           h_ref, d_ref, z_ref, ext_ref, *, tm, tiles_per_seq):
    halo = max(POOL_WINDOWS)
    width = d_ref.shape[1]
    first = pl.program_id(0) % tiles_per_seq == 0

    @pl.when(first)
    def _():
        ext_ref[0:halo, :] = umeta_ref[...]

    @pl.when(jnp.logical_not(first))
    def _():
        ext_ref[0:halo, :] = ext_ref[tm:tm + halo, :]

    y = jnp.dot(a_ref[...], wo_ref[...], preferred_element_type=_f32)
    h = _layer_norm(ALPHA * res_ref[...] + y, g_ref[...], b_ref[...])
    h_ref[...] = h
    hb = h.astype(_bf16)
    ext_ref[halo:halo + tm, :] = jnp.dot(hb, wi_ref[:, 0:width], preferred_element_type=_f32)
    z_ref[...] = jnp.dot(hb, wi_ref[:, width:2 * width],
                         preferred_element_type=_f32).astype(z_ref.dtype)

    grp = width // len(POOL_WINDOWS)
    for gi, win in enumerate(POOL_WINDOWS):
        cols = slice(gi * grp, (gi + 1) * grp)
        u = ext_ref[:, cols]
        tot, span = u, 1
        while span < win:
            tot = tot + pltpu.roll(tot, span, axis=0)
            span *= 2
        d = tot[halo:] * (1.0 / win) - u[halo:]
        d_ref[:, cols] = d.astype(d_ref.dtype)


def _layer_boundary(a, w_out, res, g, b, w_in, u_meta, *, tm, seq):
    m, d = res.shape
    halo = max(POOL_WINDOWS)
    rows = lambda: pl.BlockSpec((tm, d), lambda i: (i, 0))
    kern = functools.partial(_layer_boundary_kernel, tm=tm, tiles_per_seq=seq // tm)
    return pl.pallas_call(
        kern,
        grid=(m // tm,),
        in_specs=[rows(), _resident(w_out.shape), rows(), _resident((1, d)), _resident((1, d)),
                  _resident(w_in.shape), _resident((halo, d))],
        out_specs=[rows(), rows(), rows()],
        out_shape=[jax.ShapeDtypeStruct((m, d), _f32),
                   jax.ShapeDtypeStruct((m, d), _bf16),
                   jax.ShapeDtypeStruct((m, d), _bf16)],
        scratch_shapes=[pltpu.VMEM((tm + halo, d), _f32)],
        compiler_params=_params("arbitrary"),
        name="layer_boundary",
    )(a, w_out, res, g, b, w_in, u_meta)


def _pool_tail_kernel(d_ref, z_ref, h_ref, wg_ref, sc_ref, wo_ref, g_ref, b_ref, o_ref):
    grp = wg_ref.shape[1]
    y = None
    for gi in range(wg_ref.shape[0]):
        cols = slice(gi * grp, (gi + 1) * grp)
        e = jnp.dot(d_ref[:, cols], wg_ref[gi], preferred_element_type=_f32) * sc_ref[:, cols]
        gate = (e * _silu(z_ref[:, cols].astype(_f32))).astype(_bf16)
        part = jnp.dot(gate, wo_ref[cols, :], preferred_element_type=_f32)
        y = part if y is None else y + part
    o_ref[...] = _layer_norm(ALPHA * h_ref[...] + y, g_ref[...], b_ref[...])


def _pool_tail(dpool, z, h1, w_grp, scale, w_out, g, b, *, tm):
    m, d = h1.shape
    rows = lambda: pl.BlockSpec((tm, d), lambda i: (i, 0))
    return pl.pallas_call(
        _pool_tail_kernel,
        grid=(m // tm,),
        in_specs=[rows(), rows(), rows(), _resident(w_grp.shape), _resident((1, d)),
                  _resident((d, d)), _resident((1, d)), _resident((1, d))],
        out_specs=rows(),
        out_shape=jax.ShapeDtypeStruct((m, d), _f32),
        compiler_params=_params("arbitrary"),
        name="pool_tail",
    )(dpool, z, h1, w_grp, scale, w_out, g, b)


def kernel(x, meta_tokens, fox_w_in, fox_b_f, fox_w_out, ln0_g, ln0_b, pool_w_in, pool_w_grp,
           pool_scale, pool_w_out, ln1_g, ln1_b):
    bsz, seq, d = x.shape
    width = N_HEADS * HEAD_DIM
    m = bsz * seq

    w_in_t = fox_w_in.T
    w_in = w_in_t[:4 * width].astype(_bf16)
    wf = jnp.pad(w_in_t[4 * width:], ((0, LANES - N_HEADS), (0, 0))).astype(_bf16)
    w_out0 = fox_w_out.astype(_bf16)
    w_pin = pool_w_in.astype(_bf16)
    w_grp = pool_w_grp.astype(_bf16)
    w_out1 = pool_w_out.astype(_bf16)
    row = lambda t: t.reshape(1, d).astype(_f32)

    x2 = x.reshape(m, d)
    meta = meta_tokens.astype(_f32)

    qkvz, fl_seq = _fox_proj(x2, w_in, wf, tm=1024, tn=1024)
    qkvz_m, fl_meta = _fox_proj(meta, w_in, wf, tm=N_META, tn=1024)

    n_pad = LANES - N_META
    fl = jnp.concatenate(
        [jnp.zeros((bsz, N_HEADS, n_pad), _f32),
         jnp.broadcast_to(fl_meta[:, :N_HEADS].T[None], (bsz, N_HEADS, N_META)),
         fl_seq[:, :N_HEADS].reshape(bsz, seq, N_HEADS).transpose(0, 2, 1)], axis=2)
    c_full = _cum(fl, fox_b_f.reshape(N_HEADS, 1).astype(_f32), n_pad=n_pad)
    c_seq = c_full[:, :, LANES:].reshape(bsz, N_HEADS, 1, seq)
    c_meta = jnp.concatenate(
        [c_full[0, :, n_pad:LANES], jnp.full((N_HEADS, n_pad), NO_KEY, _f32)],
        axis=1).reshape(N_HEADS, 1, LANES)
    qkvz_m128 = jnp.pad(qkvz_m, ((0, LANES - N_META), (0, 0)))

    a = _attention(qkvz.reshape(bsz, seq, 4 * width), qkvz_m, c_seq, c_meta, tile=512)
    a_m = _meta_attention(qkvz_m128, c_meta)

    g0, b0 = row(ln0_g), row(ln0_b)
    h1_m = _out_ln(a_m, w_out0, meta, g0, b0, tm=N_META)
    u_m = _matmul(h1_m, w_pin, col0=0, n=d, out_dtype=_f32, tm=N_META, tn=1024)

    h1, dpool, z1 = _layer_boundary(a.reshape(m, width), w_out0, x2, g0, b0, w_pin, u_m,
                                    tm=256, seq=seq)
    out = _pool_tail(dpool, z1, h1, w_grp, row(pool_scale), w_out1, row(ln1_g), row(ln1_b),
                     tm=256)
    return out.reshape(bsz, seq, d)
```

```python
import functools
import math

import jax
import jax.numpy as jnp
import numpy as np
from jax import lax
from jax.experimental import pallas as pl
from jax.experimental.pallas import tpu as pltpu

N_META = 16
N_HEADS = 16
HEAD_DIM = 128
POOL_WINDOWS = (2, 4, 8, 16)
DEPTH = 2
ALPHA = (2.0 * DEPTH) ** 0.25
LN_EPS = 1e-5

LANES = 128
BF16_SUBLANES = 16
MASKED = -1e30
NO_KEY = 1e30
VMEM_LIMIT = 56 * 1024 * 1024
LOG2E = math.log2(math.e)

_NT = (((1,), (1,)), ((), ()))
_TN = (((0,), (0,)), ((), ()))
_f32 = jnp.float32
_bf16 = jnp.bfloat16


def _params(*sem):
    return pltpu.CompilerParams(dimension_semantics=sem, vmem_limit_bytes=VMEM_LIMIT)


def _fox_proj_kernel(x_ref, w_ref, wf_ref, out_ref, fl_ref, xb_ref, *, n_q_tiles, q_scale):
    j = pl.program_id(1)

    @pl.when(j == 0)
    def _():
        xb = x_ref[...].astype(_bf16)
        xb_ref[...] = xb
        wf = wf_ref[...]
        wf = jnp.concatenate([wf, jnp.zeros((LANES - wf.shape[0], wf.shape[1]), wf.dtype)], axis=0)
        fl_ref[...] = lax.dot_general(xb, wf, _NT, preferred_element_type=_f32)

    acc = lax.dot_general(xb_ref[...], w_ref[...], _NT, preferred_element_type=_f32)
    scale = jnp.where(j < n_q_tiles, q_scale, 1.0).astype(_f32)
    out_ref[...] = (acc * scale).astype(out_ref.dtype)


def _fox_proj(x, w_t, *, tm, tn):
    m, d = x.shape
    n = 4 * N_HEADS * HEAD_DIM
    kern = functools.partial(_fox_proj_kernel, n_q_tiles=(N_HEADS * HEAD_DIM) // tn,
                             q_scale=HEAD_DIM ** -0.5 * LOG2E)
    return pl.pallas_call(
        kern,
        grid=(m // tm, n // tn),
        in_specs=[
            pl.BlockSpec((tm, d), lambda i, j: (i, 0)),
            pl.BlockSpec((tn, d), lambda i, j: (j, 0)),
            pl.BlockSpec((N_HEADS, d), lambda i, j: (n // N_HEADS, 0)),
        ],
        out_specs=[
            pl.BlockSpec((tm, tn), lambda i, j: (i, j)),
            pl.BlockSpec((tm, LANES), lambda i, j: (i, 0)),
        ],
        out_shape=[
            jax.ShapeDtypeStruct((m, n), _bf16),
            jax.ShapeDtypeStruct((m, LANES), _f32),
        ],
        scratch_shapes=[pltpu.VMEM((tm, d), _bf16)],
        compiler_params=_params("arbitrary", "arbitrary"),
        name="fox_proj",
    )(x, w_t, w_t)


def _cum_kernel(fl_ref, bf_ref, c_ref, *, n_pad):
    x = fl_ref[...] + bf_ref[...]
    lf = jnp.minimum(x, 0.0) - jnp.log1p(jnp.exp(-jnp.abs(x)))
    lane = lax.broadcasted_iota(jnp.int32, x.shape, 1)
    lf = jnp.where(lane >= n_pad, lf, 0.0)
    shift = 1
    while shift < x.shape[1]:
        lf = lf + jnp.where(lane >= shift, pltpu.roll(lf, shift, axis=1), 0.0)
        shift *= 2
    c_ref[...] = lf * LOG2E


def _cum(fl, b_f, *, n_pad):
    b, h, l = fl.shape
    return pl.pallas_call(
        functools.partial(_cum_kernel, n_pad=n_pad),
        grid=(b,),
        in_specs=[
            pl.BlockSpec((None, h, l), lambda i: (i, 0, 0)),
            pl.BlockSpec((h, 1), lambda i: (0, 0)),
        ],
        out_specs=pl.BlockSpec((None, h, l), lambda i: (i, 0, 0)),
        out_shape=jax.ShapeDtypeStruct((b, h, l), _f32),
        compiler_params=_params("arbitrary"),
        name="fox_decay_cumsum",
    )(fl, b_f)


def _lane_replicated_column(row):
    return jnp.broadcast_to(row, (LANES, LANES)).T


def _silu(z):
    return z * jax.nn.sigmoid(z)


def _attention_items(n_q):
    below = [(qi, t) for qi in range(n_q) for t in range(qi)]
    diag = [(qi, qi) for qi in range(n_q)]
    return len(below), np.array(list(zip(*(below + diag))), np.int32)


def _attn_kernel(items_ref, q_ref, k_ref, v_ref, z_ref, km_ref, vm_ref, c_ref, cm_ref, o_ref,
                 kaug_sc, cmrep_sc, qt_sc, vt_sc, m_sc, l_sc, acc_sc, *pair_bufs,
                 tile, n_below):
    n_q = q_ref.shape[0] // tile
    dh = q_ref.shape[1]
    n_items = n_below + n_q
    n_lane_tiles = tile // LANES

    def bf16_pieces(x):
        hi = x.astype(_bf16).astype(_f32)
        mid = (x - hi).astype(_bf16).astype(_f32)
        lo = (x - hi - mid).astype(_bf16).astype(_f32)
        return hi, mid, lo

    def stack_rows(shape, rows):
        sub = lax.broadcasted_iota(jnp.int32, shape, 0)
        out = jnp.zeros(shape, _f32)
        for r, row in enumerate(rows):
            out = jnp.where(sub == r, row, out)
        return out

    neg_c = bf16_pieces(-c_ref[...])
    ones = jnp.ones_like(neg_c[0])
    key_terms = stack_rows((8, c_ref.shape[1]), [*neg_c, ones, ones, ones])
    pad = jnp.zeros((LANES - 8, LANES), _f32)
    kaug_sc[:, 0:dh] = k_ref[...]
    for ch in range(c_ref.shape[1] // LANES):
        rows = slice(ch * LANES, (ch + 1) * LANES)
        block = jnp.concatenate([key_terms[:, rows], pad], axis=0).T
        kaug_sc[rows, dh:] = block.astype(_bf16)
    cmrep_sc[...] = _lane_replicated_column(cm_ref[...])

    for qi in range(n_q):
        rows = slice(qi * tile, (qi + 1) * tile)
        c_shift = jnp.min(c_ref[:, rows], axis=1, keepdims=True)
        one = jnp.ones((1, tile), _f32)
        shift = [jnp.broadcast_to(piece, (1, tile)) for piece in bf16_pieces(c_shift)]
        qt_sc[qi, 0:dh, :] = q_ref[rows, :].T
        qt_sc[qi, dh:, :] = stack_rows((dh, tile), [one, one, one, *shift]).astype(_bf16)
        vt_sc[qi, 0:dh, :] = v_ref[rows, :].T
        vt_sc[qi, dh:, :] = jnp.ones((vt_sc.shape[1] - dh, tile), _bf16)
        bias = c_shift - cmrep_sc[0:N_META, :]
        s = jnp.dot(km_ref[...], qt_sc[qi, 0:dh, :], preferred_element_type=_f32)
        s = s + jnp.tile(bias, (1, n_lane_tiles))
        m = jnp.max(s, axis=0, keepdims=True)
        p = jnp.exp2(s - m)
        m_sc[qi] = m
        l_sc[qi] = jnp.sum(p, axis=0, keepdims=True)
        acc_sc[qi] = lax.dot_general(vm_ref[...], p.astype(_bf16), _TN,
                                     preferred_element_type=_f32)

    s_bufs, p_bufs, alpha_bufs, smax_bufs = (pair_bufs[4 * n:4 * n + 4] for n in range(4))

    def tile_rows(t):
        return pl.ds(pl.multiple_of(t * tile, tile), tile)

    def stage_scores(i, buf, masked):
        qi, rows = items_ref[0, i], tile_rows(items_ref[1, i])
        s = jnp.dot(kaug_sc[rows, :], qt_sc[qi], preferred_element_type=_f32)
        if masked:
            key = lax.broadcasted_iota(jnp.int32, s.shape, 0)
            qry = lax.broadcasted_iota(jnp.int32, s.shape, 1)
            s = jnp.where(key <= qry, s, MASKED)
        s_bufs[buf][...] = s
        smax_bufs[buf][...] = jnp.max(s, axis=0, keepdims=True)

    def stage_softmax(i, buf):
        qi = items_ref[0, i]
        s = s_bufs[buf][...]
        m_prev = m_sc[qi]
        m_new = jnp.maximum(m_prev, smax_bufs[buf][...])
        alpha = jnp.exp2(m_prev - m_new)
        p = jnp.exp2(s - m_new)
        m_sc[qi] = m_new
        p_bufs[buf][...] = p.astype(_bf16)
        alpha_bufs[buf][...] = alpha

    def stage_pv(i, buf):
        qi, t = items_ref[0, i], items_ref[1, i]
        alpha = alpha_bufs[buf][...]
        pv = jnp.dot(vt_sc[t], p_bufs[buf][...], preferred_element_type=_f32)
        acc_sc[qi] = alpha * acc_sc[qi] + pv[0:dh]
        l_sc[qi] = alpha * l_sc[qi] + pv[dh:dh + 1]

    def block(j, parity, *, scores=None, softmax=True, pv=True):
        for slot in range(2):
            if scores is not None:
                stage_scores(2 * j + slot, 2 * parity + slot, scores)
            if softmax:
                stage_softmax(2 * (j - 1) + slot, 2 * (1 - parity) + slot)
            if pv:
                stage_pv(2 * (j - 2) + slot, 2 * parity + slot)

    n_pairs = n_items // 2
    first_diag_pair = n_below // 2
    block(0, 0, scores=False, softmax=False, pv=False)
    block(1, 1, scores=False, pv=False)

    def steady(masked, first_block):
        def body(u, carry):
            block(first_block + 2 * u, 0, scores=masked)
            block(first_block + 2 * u + 1, 1, scores=masked)
            return carry
        return body

    lax.fori_loop(0, (first_diag_pair - 2) // 2, steady(False, 2), 0)
    lax.fori_loop(0, (n_pairs - first_diag_pair) // 2, steady(True, first_diag_pair), 0)
    block(n_pairs, 0)
    block(n_pairs + 1, 1, softmax=False)

    for qi in range(n_q):
        rows = slice(qi * tile, (qi + 1) * tile)
        o = (acc_sc[qi] * (1.0 / l_sc[qi])).T
        o_ref[rows, :] = (o * _silu(z_ref[rows, :].astype(_f32))).astype(o_ref.dtype)


def _attention(qkvz, qkvz_meta, c, c_meta, *, tile):
    b, l, _ = qkvz.shape
    h, dh = N_HEADS, HEAD_DIM
    n_q = l // tile
    n_below, items = _attention_items(n_q)
    assert n_below % 4 == 0 and n_q % 4 == 0 and n_below >= 8
    head_cols = lambda first: pl.BlockSpec((None, l, dh), lambda bi, hi, _: (bi, 0, first + hi))
    meta_cols = lambda first: pl.BlockSpec((N_META, dh), lambda bi, hi, _: (0, first + hi))
    grid_spec = pltpu.PrefetchScalarGridSpec(
        num_scalar_prefetch=1,
        grid=(b, h),
        in_specs=[
            head_cols(0), head_cols(h), head_cols(2 * h), head_cols(3 * h),
            meta_cols(h), meta_cols(2 * h),
            pl.BlockSpec((None, None, 1, l), lambda bi, hi, _: (bi, hi, 0, 0)),
            pl.BlockSpec((None, 1, LANES), lambda bi, hi, _: (hi, 0, 0)),
        ],
        out_specs=pl.BlockSpec((None, l, dh), lambda bi, hi, _: (bi, 0, hi)),
        scratch_shapes=[
            pltpu.VMEM((l, 2 * dh), _bf16),
            pltpu.VMEM((LANES, LANES), _f32),
            pltpu.VMEM((n_q, 2 * dh, tile), _bf16),
            pltpu.VMEM((n_q, dh + BF16_SUBLANES, tile), _bf16),
            pltpu.VMEM((n_q, 1, tile), _f32),
            pltpu.VMEM((n_q, 1, tile), _f32),
            pltpu.VMEM((n_q, dh, tile), _f32),
        ] + [pltpu.VMEM((tile, tile), _f32)] * 4
          + [pltpu.VMEM((tile, tile), _bf16)] * 4
          + [pltpu.VMEM((1, tile), _f32)] * 4
          + [pltpu.VMEM((1, tile), _f32)] * 4,
    )
    return pl.pallas_call(
        functools.partial(_attn_kernel, tile=tile, n_below=n_below),
        grid_spec=grid_spec,
        out_shape=jax.ShapeDtypeStruct((b, l, h * dh), _bf16),
        compiler_params=_params("arbitrary", "arbitrary"),
        name="fox_attention",
    )(jnp.asarray(items), qkvz, qkvz, qkvz, qkvz, qkvz_meta, qkvz_meta, c, c_meta)


def _meta_attn_kernel(qkvz_ref, c_ref, o_ref):
    w = N_HEADS * HEAD_DIM
    row = lax.broadcasted_iota(jnp.int32, (N_META, LANES), 0)
    col = lax.broadcasted_iota(jnp.int32, (N_META, LANES), 1)
    for hd in range(N_HEADS):
        lo = hd * HEAD_DIM
        q = qkvz_ref[0:N_META, lo:lo + HEAD_DIM]
        k = qkvz_ref[:, w + lo:w + lo + HEAD_DIM]
        v = qkvz_ref[:, 2 * w + lo:2 * w + lo + HEAD_DIM]
        z = qkvz_ref[0:N_META, 3 * w + lo:3 * w + lo + HEAD_DIM]
        ck = c_ref[hd]
        c_shift = jnp.min(jnp.where(col[0:1] < N_META, ck, NO_KEY), axis=1, keepdims=True)
        s = lax.dot_general(q, k, _NT, preferred_element_type=_f32) + (c_shift - ck)
        s = jnp.where(col <= row, s, MASKED)
        m = jnp.max(s, axis=1, keepdims=True)
        p = jnp.exp2(s - m)
        o = jnp.dot(p.astype(_bf16), v, preferred_element_type=_f32)
        o = o / jnp.sum(p, axis=1, keepdims=True)
        o_ref[:, lo:lo + HEAD_DIM] = (o * _silu(z.astype(_f32))).astype(o_ref.dtype)


def _meta_attention(qkvz_meta, c_meta):
    return pl.pallas_call(
        _meta_attn_kernel,
        out_shape=jax.ShapeDtypeStruct((N_META, N_HEADS * HEAD_DIM), _bf16),
        compiler_params=pltpu.CompilerParams(vmem_limit_bytes=VMEM_LIMIT),
        name="fox_meta_attention",
    )(qkvz_meta, c_meta)


def _layer_norm(r, g, b):
    mu = jnp.mean(r, axis=-1, keepdims=True)
    d = r - mu
    var = jnp.mean(d * d, axis=-1, keepdims=True)
    return d * lax.rsqrt(var + LN_EPS) * g + b


def _out_ln_kernel(a_ref, w_ref, res_ref, g_ref, b_ref, o_ref):
    y = jnp.dot(a_ref[...], w_ref[...], preferred_element_type=_f32)
    o_ref[...] = _layer_norm(ALPHA * res_ref[...] + y, g_ref[...], b_ref[...])


def _resident(shape):
    nd = len(shape)
    return pl.BlockSpec(shape, lambda *_: (0,) * nd, pipeline_mode=pl.Buffered(1))


def _out_ln(a, w, res, g, b, *, tm):
    m, d = res.shape
    k = a.shape[1]
    return pl.pallas_call(
        _out_ln_kernel,
        grid=(m // tm,),
        in_specs=[
            pl.BlockSpec((tm, k), lambda i: (i, 0)),
            _resident((k, d)),
            pl.BlockSpec((tm, d), lambda i: (i, 0)),
            _resident((1, d)),
            _resident((1, d)),
        ],
        out_specs=pl.BlockSpec((tm, d), lambda i: (i, 0)),
        out_shape=jax.ShapeDtypeStruct((m, d), _f32),
        compiler_params=_params("arbitrary"),
        name="out_proj_layernorm",
    )(a, w, res, g, b)


def _matmul_kernel(x_ref, w_ref, o_ref, xb_ref):
    @pl.when(pl.program_id(1) == 0)
    def _():
        xb_ref[...] = x_ref[...].astype(_bf16)

    o_ref[...] = jnp.dot(xb_ref[...], w_ref[...], preferred_element_type=_f32).astype(o_ref.dtype)


def _matmul(x, w, *, col0, n, out_dtype, tm, tn):
    m, d = x.shape
    off = col0 // tn
    return pl.pallas_call(
        _matmul_kernel,
        grid=(m // tm, n // tn),
        in_specs=[
            pl.BlockSpec((tm, d), lambda i, j: (i, 0)),
            pl.BlockSpec((d, tn), lambda i, j: (0, off + j)),
        ],
        out_specs=pl.BlockSpec((tm, tn), lambda i, j: (i, j)),
        out_shape=jax.ShapeDtypeStruct((m, n), out_dtype),
        scratch_shapes=[pltpu.VMEM((tm, d), _bf16)],
        compiler_params=_params("arbitrary", "arbitrary"),
        name="pool_in_proj",
    )(x, w)


def _layer_boundary_kernel(a_ref, wo_ref, res_ref, g_ref, b_ref, wi_ref, umeta_ref,
                           h_ref, d_ref, z_ref, ext_ref, *, tm, tiles_per_seq):
    halo = max(POOL_WINDOWS)
    width = d_ref.shape[1]
    first = pl.program_id(0) % tiles_per_seq == 0

    @pl.when(first)
    def _():
        ext_ref[0:halo, :] = umeta_ref[...]

    @pl.when(jnp.logical_not(first))
    def _():
        ext_ref[0:halo, :] = ext_ref[tm:tm + halo, :]

    y = jnp.dot(a_ref[...], wo_ref[...], preferred_element_type=_f32)
    h = _layer_norm(ALPHA * res_ref[...] + y, g_ref[...], b_ref[...])
    h_ref[...] = h
    hb = h.astype(_bf16)
    ext_ref[halo:halo + tm, :] = jnp.dot(hb, wi_ref[:, 0:width], preferred_element_type=_f32)
    z_ref[...] = jnp.dot(hb, wi_ref[:, width:2 * width],
                         preferred_element_type=_f32).astype(z_ref.dtype)

    grp = width // len(POOL_WINDOWS)
    for gi, win in enumerate(POOL_WINDOWS):
        cols = slice(gi * grp, (gi + 1) * grp)
        u = ext_ref[:, cols]
        tot, span = u, 1
        while span < win:
            tot = tot + pltpu.roll(tot, span, axis=0)
            span *= 2
        d = tot[halo:] * (1.0 / win) - u[halo:]
        d_ref[:, cols] = d.astype(d_ref.dtype)


def _layer_boundary(a, w_out, res, g, b, w_in, u_meta, *, tm, seq):
    m, d = res.shape
    halo = max(POOL_WINDOWS)
    rows = lambda: pl.BlockSpec((tm, d), lambda i: (i, 0))
    kern = functools.partial(_layer_boundary_kernel, tm=tm, tiles_per_seq=seq // tm)
    return pl.pallas_call(
        kern,
        grid=(m // tm,),
        in_specs=[rows(), _resident(w_out.shape), rows(), _resident((1, d)), _resident((1, d)),
                  _resident(w_in.shape), _resident((halo, d))],
        out_specs=[rows(), rows(), rows()],
        out_shape=[jax.ShapeDtypeStruct((m, d), _f32),
                   jax.ShapeDtypeStruct((m, d), _bf16),
                   jax.ShapeDtypeStruct((m, d), _bf16)],
        scratch_shapes=[pltpu.VMEM((tm + halo, d), _f32)],
        compiler_params=_params("arbitrary"),
        name="layer_boundary",
    )(a, w_out, res, g, b, w_in, u_meta)


def _pool_tail_kernel(d_ref, z_ref, h_ref, wg_ref, sc_ref, wo_ref, g_ref, b_ref, o_ref):
    grp = wg_ref.shape[1]
    y = None
    for gi in range(wg_ref.shape[0]):
        cols = slice(gi * grp, (gi + 1) * grp)
        e = jnp.dot(d_ref[:, cols], wg_ref[gi], preferred_element_type=_f32) * sc_ref[:, cols]
        gate = (e * _silu(z_ref[:, cols].astype(_f32))).astype(_bf16)
        part = jnp.dot(gate, wo_ref[cols, :], preferred_element_type=_f32)
        y = part if y is None else y + part
    o_ref[...] = _layer_norm(ALPHA * h_ref[...] + y, g_ref[...], b_ref[...])


def _pool_tail(dpool, z, h1, w_grp, scale, w_out, g, b, *, tm):
    m, d = h1.shape
    rows = lambda: pl.BlockSpec((tm, d), lambda i: (i, 0))
    return pl.pallas_call(
        _pool_tail_kernel,
        grid=(m // tm,),
        in_specs=[rows(), rows(), rows(), _resident(w_grp.shape), _resident((1, d)),
                  _resident((d, d)), _resident((1, d)), _resident((1, d))],
        out_specs=rows(),
        out_shape=jax.ShapeDtypeStruct((m, d), _f32),
        compiler_params=_params("arbitrary"),
        name="pool_tail",
    )(dpool, z, h1, w_grp, scale, w_out, g, b)


def kernel(x, meta_tokens, fox_w_in, fox_b_f, fox_w_out, ln0_g, ln0_b, pool_w_in, pool_w_grp,
           pool_scale, pool_w_out, ln1_g, ln1_b):
    bsz, seq, d = x.shape
    width = N_HEADS * HEAD_DIM
    m = bsz * seq

    w_in_t = fox_w_in.T.astype(_bf16)
    w_out0 = fox_w_out.astype(_bf16)
    w_pin = pool_w_in.astype(_bf16)
    w_grp = pool_w_grp.astype(_bf16)
    w_out1 = pool_w_out.astype(_bf16)
    row = lambda t: t.reshape(1, d).astype(_f32)

    x2 = x.reshape(m, d)
    meta = meta_tokens.astype(_f32)

    qkvz, fl_seq = _fox_proj(x2, w_in_t, tm=1024, tn=1024)
    qkvz_m, fl_meta = _fox_proj(meta, w_in_t, tm=N_META, tn=1024)

    n_pad = LANES - N_META
    fl = jnp.concatenate(
        [jnp.zeros((bsz, N_HEADS, n_pad), _f32),
         jnp.broadcast_to(fl_meta[:, :N_HEADS].T[None], (bsz, N_HEADS, N_META)),
         fl_seq[:, :N_HEADS].reshape(bsz, seq, N_HEADS).transpose(0, 2, 1)], axis=2)
    c_full = _cum(fl, fox_b_f.reshape(N_HEADS, 1).astype(_f32), n_pad=n_pad)
    c_seq = c_full[:, :, LANES:].reshape(bsz, N_HEADS, 1, seq)
    c_meta = jnp.concatenate(
        [c_full[0, :, n_pad:LANES], jnp.full((N_HEADS, n_pad), NO_KEY, _f32)],
        axis=1).reshape(N_HEADS, 1, LANES)
    qkvz_m128 = jnp.pad(qkvz_m, ((0, LANES - N_META), (0, 0)))

    a = _attention(qkvz.reshape(bsz, seq, 4 * width), qkvz_m, c_seq, c_meta, tile=512)
    a_m = _meta_attention(qkvz_m128, c_meta)

    g0, b0 = row(ln0_g), row(ln0_b)
    h1_m = _out_ln(a_m, w_out0, meta, g0, b0, tm=N_META)
    u_m = _matmul(h1_m, w_pin, col0=0, n=d, out_dtype=_f32, tm=N_META, tn=1024)

    h1, dpool, z1 = _layer_boundary(a.reshape(m, width), w_out0, x2, g0, b0, w_pin, u_m,
                                    tm=256, seq=seq)
    out = _pool_tail(dpool, z1, h1, w_grp, row(pool_scale), w_out1, row(ln1_g), row(ln1_b),
                     tm=256)
    return out.reshape(bsz, seq, d)
```

```python
import functools
import math

import jax
import jax.numpy as jnp
from jax import lax
from jax.experimental import pallas as pl
from jax.experimental.pallas import tpu as pltpu

N_META = 16
N_HEADS = 16
HEAD_DIM = 128
POOL_WINDOWS = (2, 4, 8, 16)
DEPTH = 2
ALPHA = (2.0 * DEPTH) ** 0.25
LN_EPS = 1e-5

LANES = 128
BF16_SUBLANES = 16
MASKED = -1e30
NO_KEY = 1e30
VMEM_LIMIT = 56 * 1024 * 1024
LOG2E = math.log2(math.e)

_NT = (((1,), (1,)), ((), ()))
_TN = (((0,), (0,)), ((), ()))
_f32 = jnp.float32
_bf16 = jnp.bfloat16


def _params(*sem):
    return pltpu.CompilerParams(dimension_semantics=sem, vmem_limit_bytes=VMEM_LIMIT)


def _fox_proj_kernel(x_ref, w_ref, wf_ref, out_ref, fl_ref, xb_ref, *, n_q_tiles, q_scale):
    j = pl.program_id(1)

    @pl.when(j == 0)
    def _():
        xb = x_ref[...].astype(_bf16)
        xb_ref[...] = xb
        wf = wf_ref[...]
        wf = jnp.concatenate([wf, jnp.zeros((LANES - wf.shape[0], wf.shape[1]), wf.dtype)], axis=0)
        fl_ref[...] = lax.dot_general(xb, wf, _NT, preferred_element_type=_f32)

    acc = lax.dot_general(xb_ref[...], w_ref[...], _NT, preferred_element_type=_f32)
    scale = jnp.where(j < n_q_tiles, q_scale, 1.0).astype(_f32)
    out_ref[...] = (acc * scale).astype(out_ref.dtype)


def _fox_proj(x, w_t, *, tm, tn):
    m, d = x.shape
    n = 4 * N_HEADS * HEAD_DIM
    kern = functools.partial(_fox_proj_kernel, n_q_tiles=(N_HEADS * HEAD_DIM) // tn,
                             q_scale=HEAD_DIM ** -0.5 * LOG2E)
    return pl.pallas_call(
        kern,
        grid=(m // tm, n // tn),
        in_specs=[
            pl.BlockSpec((tm, d), lambda i, j: (i, 0)),
            pl.BlockSpec((tn, d), lambda i, j: (j, 0)),
            pl.BlockSpec((N_HEADS, d), lambda i, j: (n // N_HEADS, 0)),
        ],
        out_specs=[
            pl.BlockSpec((tm, tn), lambda i, j: (i, j)),
            pl.BlockSpec((tm, LANES), lambda i, j: (i, 0)),
        ],
        out_shape=[
            jax.ShapeDtypeStruct((m, n), _bf16),
            jax.ShapeDtypeStruct((m, LANES), _f32),
        ],
        scratch_shapes=[pltpu.VMEM((tm, d), _bf16)],
        compiler_params=_params("arbitrary", "arbitrary"),
        name="fox_proj",
    )(x, w_t, w_t)


def _cum_kernel(fl_ref, bf_ref, c_ref, *, n_pad):
    x = fl_ref[...] + bf_ref[...]
    lf = jnp.minimum(x, 0.0) - jnp.log1p(jnp.exp(-jnp.abs(x)))
    lane = lax.broadcasted_iota(jnp.int32, x.shape, 1)
    lf = jnp.where(lane >= n_pad, lf, 0.0)
    shift = 1
    while shift < x.shape[1]:
        lf = lf + jnp.where(lane >= shift, pltpu.roll(lf, shift, axis=1), 0.0)
        shift *= 2
    c_ref[...] = lf * LOG2E


def _cum(fl, b_f, *, n_pad):
    b, h, l = fl.shape
    return pl.pallas_call(
        functools.partial(_cum_kernel, n_pad=n_pad),
        grid=(b,),
        in_specs=[
            pl.BlockSpec((None, h, l), lambda i: (i, 0, 0)),
            pl.BlockSpec((h, 1), lambda i: (0, 0)),
        ],
        out_specs=pl.BlockSpec((None, h, l), lambda i: (i, 0, 0)),
        out_shape=jax.ShapeDtypeStruct((b, h, l), _f32),
        compiler_params=_params("arbitrary"),
        name="fox_decay_cumsum",
    )(fl, b_f)


def _lane_replicated_column(row):
    return jnp.broadcast_to(row, (LANES, LANES)).T


def _silu(z):
    return z * jax.nn.sigmoid(z)


def _attention_items(n_q):
    below = [(qi, t) for qi in range(n_q) for t in range(qi)]
    diag = [(qi, qi) for qi in range(n_q)]
    return len(below), below + diag


def _attn_kernel(q_ref, k_ref, v_ref, z_ref, km_ref, vm_ref, c_ref, cm_ref, o_ref,
                 kaug_sc, cmrep_sc, qt_sc, vt_sc, m_sc, l_sc, acc_sc, *pair_bufs, tile):
    n_q = q_ref.shape[0] // tile
    dh = q_ref.shape[1]
    n_below, items = _attention_items(n_q)
    n_items = len(items)
    n_lane_tiles = tile // LANES

    def bf16_pieces(x):
        hi = x.astype(_bf16).astype(_f32)
        mid = (x - hi).astype(_bf16).astype(_f32)
        lo = (x - hi - mid).astype(_bf16).astype(_f32)
        return hi, mid, lo

    def stack_rows(shape, rows):
        sub = lax.broadcasted_iota(jnp.int32, shape, 0)
        out = jnp.zeros(shape, _f32)
        for r, row in enumerate(rows):
            out = jnp.where(sub == r, row, out)
        return out

    neg_c = bf16_pieces(-c_ref[...])
    ones = jnp.ones_like(neg_c[0])
    key_terms = stack_rows((8, c_ref.shape[1]), [*neg_c, ones, ones, ones])
    pad = jnp.zeros((LANES - 8, LANES), _f32)
    kaug_sc[:, 0:dh] = k_ref[...]
    for ch in range(c_ref.shape[1] // LANES):
        rows = slice(ch * LANES, (ch + 1) * LANES)
        block = jnp.concatenate([key_terms[:, rows], pad], axis=0).T
        kaug_sc[rows, dh:] = block.astype(_bf16)
    cmrep_sc[...] = _lane_replicated_column(cm_ref[...])

    for qi in range(n_q):
        rows = slice(qi * tile, (qi + 1) * tile)
        c_shift = jnp.min(c_ref[:, rows], axis=1, keepdims=True)
        one = jnp.ones((1, tile), _f32)
        shift = [jnp.broadcast_to(piece, (1, tile)) for piece in bf16_pieces(c_shift)]
        qt_sc[qi, 0:dh, :] = q_ref[rows, :].T
        qt_sc[qi, dh:, :] = stack_rows((dh, tile), [one, one, one, *shift]).astype(_bf16)
        vt_sc[qi, 0:dh, :] = v_ref[rows, :].T
        vt_sc[qi, dh:, :] = jnp.ones((vt_sc.shape[1] - dh, tile), _bf16)
        bias = c_shift - cmrep_sc[0:N_META, :]
        s = jnp.dot(km_ref[...], qt_sc[qi, 0:dh, :], preferred_element_type=_f32)
        s = s + jnp.tile(bias, (1, n_lane_tiles))
        m = jnp.max(s, axis=0, keepdims=True)
        p = jnp.exp2(s - m)
        m_sc[qi] = m
        l_sc[qi] = jnp.sum(p, axis=0, keepdims=True)
        acc_sc[qi] = lax.dot_general(vm_ref[...], p.astype(_bf16), _TN,
                                     preferred_element_type=_f32)

    s_bufs, p_bufs, alpha_bufs, smax_bufs = (pair_bufs[4 * n:4 * n + 4] for n in range(4))

    def stage_scores(i, buf):
        qi, t = items[i]
        s = jnp.dot(kaug_sc[t * tile:(t + 1) * tile, :], qt_sc[qi], preferred_element_type=_f32)
        if t == qi:
            key = lax.broadcasted_iota(jnp.int32, s.shape, 0)
            qry = lax.broadcasted_iota(jnp.int32, s.shape, 1)
            s = jnp.where(key <= qry, s, MASKED)
        s_bufs[buf][...] = s
        smax_bufs[buf][...] = jnp.max(s, axis=0, keepdims=True)

    def stage_softmax(i, buf):
        qi, _ = items[i]
        s = s_bufs[buf][...]
        m_prev = m_sc[qi]
        m_new = jnp.maximum(m_prev, smax_bufs[buf][...])
        alpha = jnp.exp2(m_prev - m_new)
        p = jnp.exp2(s - m_new)
        m_sc[qi] = m_new
        p_bufs[buf][...] = p.astype(_bf16)
        alpha_bufs[buf][...] = alpha

    def stage_pv(i, buf):
        qi, t = items[i]
        alpha = alpha_bufs[buf][...]
        pv = jnp.dot(vt_sc[t], p_bufs[buf][...], preferred_element_type=_f32)
        acc_sc[qi] = alpha * acc_sc[qi] + pv[0:dh]
        l_sc[qi] = alpha * l_sc[qi] + pv[dh:dh + 1]

    n_pairs = n_items // 2
    for j in range(n_pairs + 2):
        parity = j % 2
        for slot in range(2):
            if j < n_pairs:
                stage_scores(2 * j + slot, 2 * parity + slot)
            if 1 <= j <= n_pairs:
                stage_softmax(2 * (j - 1) + slot, 2 * (1 - parity) + slot)
            if j >= 2:
                stage_pv(2 * (j - 2) + slot, 2 * parity + slot)

    for qi in range(n_q):
        rows = slice(qi * tile, (qi + 1) * tile)
        o = (acc_sc[qi] * (1.0 / l_sc[qi])).T
        o_ref[rows, :] = (o * _silu(z_ref[rows, :].astype(_f32))).astype(o_ref.dtype)


def _attention(qkvz, qkvz_meta, c, c_meta, *, tile):
    b, l, _ = qkvz.shape
    h, dh = N_HEADS, HEAD_DIM
    n_q = l // tile
    assert len(_attention_items(n_q)[1]) % 2 == 0
    head_cols = lambda first: pl.BlockSpec((None, l, dh), lambda bi, hi: (bi, 0, first + hi))
    meta_cols = lambda first: pl.BlockSpec((N_META, dh), lambda bi, hi: (0, first + hi))
    return pl.pallas_call(
        functools.partial(_attn_kernel, tile=tile),
        grid=(b, h),
        in_specs=[
            head_cols(0), head_cols(h), head_cols(2 * h), head_cols(3 * h),
            meta_cols(h), meta_cols(2 * h),
            pl.BlockSpec((None, None, 1, l), lambda bi, hi: (bi, hi, 0, 0)),
            pl.BlockSpec((None, 1, LANES), lambda bi, hi: (hi, 0, 0)),
        ],
        out_specs=pl.BlockSpec((None, l, dh), lambda bi, hi: (bi, 0, hi)),
        scratch_shapes=[
            pltpu.VMEM((l, 2 * dh), _bf16),
            pltpu.VMEM((LANES, LANES), _f32),
            pltpu.VMEM((n_q, 2 * dh, tile), _bf16),
            pltpu.VMEM((n_q, dh + BF16_SUBLANES, tile), _bf16),
            pltpu.VMEM((n_q, 1, tile), _f32),
            pltpu.VMEM((n_q, 1, tile), _f32),
            pltpu.VMEM((n_q, dh, tile), _f32),
        ] + [pltpu.VMEM((tile, tile), _f32)] * 4
          + [pltpu.VMEM((tile, tile), _bf16)] * 4
          + [pltpu.VMEM((1, tile), _f32)] * 4
          + [pltpu.VMEM((1, tile), _f32)] * 4,
        out_shape=jax.ShapeDtypeStruct((b, l, h * dh), _bf16),
        compiler_params=_params("arbitrary", "arbitrary"),
        name="fox_attention",
    )(qkvz, qkvz, qkvz, qkvz, qkvz_meta, qkvz_meta, c, c_meta)


def _meta_attn_kernel(qkvz_ref, c_ref, o_ref):
    w = N_HEADS * HEAD_DIM
    row = lax.broadcasted_iota(jnp.int32, (N_META, LANES), 0)
    col = lax.broadcasted_iota(jnp.int32, (N_META, LANES), 1)
    for hd in range(N_HEADS):
        lo = hd * HEAD_DIM
        q = qkvz_ref[0:N_META, lo:lo + HEAD_DIM]
        k = qkvz_ref[:, w + lo:w + lo + HEAD_DIM]
        v = qkvz_ref[:, 2 * w + lo:2 * w + lo + HEAD_DIM]
        z = qkvz_ref[0:N_META, 3 * w + lo:3 * w + lo + HEAD_DIM]
        ck = c_ref[hd]
        c_shift = jnp.min(jnp.where(col[0:1] < N_META, ck, NO_KEY), axis=1, keepdims=True)
        s = lax.dot_general(q, k, _NT, preferred_element_type=_f32) + (c_shift - ck)
        s = jnp.where(col <= row, s, MASKED)
        m = jnp.max(s, axis=1, keepdims=True)
        p = jnp.exp2(s - m)
        o = jnp.dot(p.astype(_bf16), v, preferred_element_type=_f32)
        o = o / jnp.sum(p, axis=1, keepdims=True)
        o_ref[:, lo:lo + HEAD_DIM] = (o * _silu(z.astype(_f32))).astype(o_ref.dtype)


def _meta_attention(qkvz_meta, c_meta):
    return pl.pallas_call(
        _meta_attn_kernel,
        out_shape=jax.ShapeDtypeStruct((N_META, N_HEADS * HEAD_DIM), _bf16),
        compiler_params=pltpu.CompilerParams(vmem_limit_bytes=VMEM_LIMIT),
        name="fox_meta_attention",
    )(qkvz_meta, c_meta)


def _layer_norm(r, g, b):
    mu = jnp.mean(r, axis=-1, keepdims=True)
    d = r - mu
    var = jnp.mean(d * d, axis=-1, keepdims=True)
    return d * lax.rsqrt(var + LN_EPS) * g + b


def _out_ln_kernel(a_ref, w_ref, res_ref, g_ref, b_ref, o_ref):
    y = jnp.dot(a_ref[...], w_ref[...], preferred_element_type=_f32)
    o_ref[...] = _layer_norm(ALPHA * res_ref[...] + y, g_ref[...], b_ref[...])


def _resident(shape):
    nd = len(shape)
    return pl.BlockSpec(shape, lambda *_: (0,) * nd, pipeline_mode=pl.Buffered(1))


def _out_ln(a, w, res, g, b, *, tm):
    m, d = res.shape
    k = a.shape[1]
    return pl.pallas_call(
        _out_ln_kernel,
        grid=(m // tm,),
        in_specs=[
            pl.BlockSpec((tm, k), lambda i: (i, 0)),
            _resident((k, d)),
            pl.BlockSpec((tm, d), lambda i: (i, 0)),
            _resident((1, d)),
            _resident((1, d)),
        ],
        out_specs=pl.BlockSpec((tm, d), lambda i: (i, 0)),
        out_shape=jax.ShapeDtypeStruct((m, d), _f32),
        compiler_params=_params("arbitrary"),
        name="out_proj_layernorm",
    )(a, w, res, g, b)


def _matmul_kernel(x_ref, w_ref, o_ref, xb_ref):
    @pl.when(pl.program_id(1) == 0)
    def _():
        xb_ref[...] = x_ref[...].astype(_bf16)

    o_ref[...] = jnp.dot(xb_ref[...], w_ref[...], preferred_element_type=_f32).astype(o_ref.dtype)


def _matmul(x, w, *, col0, n, out_dtype, tm, tn):
    m, d = x.shape
    off = col0 // tn
    return pl.pallas_call(
        _matmul_kernel,
        grid=(m // tm, n // tn),
        in_specs=[
            pl.BlockSpec((tm, d), lambda i, j: (i, 0)),
            pl.BlockSpec((d, tn), lambda i, j: (0, off + j)),
        ],
        out_specs=pl.BlockSpec((tm, tn), lambda i, j: (i, j)),
        out_shape=jax.ShapeDtypeStruct((m, n), out_dtype),
        scratch_shapes=[pltpu.VMEM((tm, d), _bf16)],
        compiler_params=_params("arbitrary", "arbitrary"),
        name="pool_in_proj",
    )(x, w)


def _layer_boundary_kernel(a_ref, wo_ref, res_ref, g_ref, b_ref, wi_ref, umeta_ref,
                           h_ref, d_ref, z_ref, ext_ref, *, tm, tiles_per_seq):
    halo = max(POOL_WINDOWS)
    width = d_ref.shape[1]
    first = pl.program_id(0) % tiles_per_seq == 0

    @pl.when(first)
    def _():
        ext_ref[0:halo, :] = umeta_ref[...]

    @pl.when(jnp.logical_not(first))
    def _():
        ext_ref[0:halo, :] = ext_ref[tm:tm + halo, :]

    y = jnp.dot(a_ref[...], wo_ref[...], preferred_element_type=_f32)
    h = _layer_norm(ALPHA * res_ref[...] + y, g_ref[...], b_ref[...])
    h_ref[...] = h
    hb = h.astype(_bf16)
    ext_ref[halo:halo + tm, :] = jnp.dot(hb, wi_ref[:, 0:width], preferred_element_type=_f32)
    z_ref[...] = jnp.dot(hb, wi_ref[:, width:2 * width],
                         preferred_element_type=_f32).astype(z_ref.dtype)

    grp = width // len(POOL_WINDOWS)
    for gi, win in enumerate(POOL_WINDOWS):
        cols = slice(gi * grp, (gi + 1) * grp)
        u = ext_ref[:, cols]
        tot, span = u, 1
        while span < win:
            tot = tot + pltpu.roll(tot, span, axis=0)
            span *= 2
        d = tot[halo:] * (1.0 / win) - u[halo:]
        d_ref[:, cols] = d.astype(d_ref.dtype)


def _layer_boundary(a, w_out, res, g, b, w_in, u_meta, *, tm, seq):
    m, d = res.shape
    halo = max(POOL_WINDOWS)
    rows = lambda: pl.BlockSpec((tm, d), lambda i: (i, 0))
    kern = functools.partial(_layer_boundary_kernel, tm=tm, tiles_per_seq=seq // tm)
    return pl.pallas_call(
        kern,
        grid=(m // tm,),
        in_specs=[rows(), _resident(w_out.shape), rows(), _resident((1, d)), _resident((1, d)),
                  _resident(w_in.shape), _resident((halo, d))],
        out_specs=[rows(), rows(), rows()],
        out_shape=[jax.ShapeDtypeStruct((m, d), _f32),
                   jax.ShapeDtypeStruct((m, d), _bf16),
                   jax.ShapeDtypeStruct((m, d), _bf16)],
        scratch_shapes=[pltpu.VMEM((tm + halo, d), _f32)],
        compiler_params=_params("arbitrary"),
        name="layer_boundary",
    )(a, w_out, res, g, b, w_in, u_meta)


def _pool_tail_kernel(d_ref, z_ref, h_ref, wg_ref, sc_ref, wo_ref, g_ref, b_ref, o_ref):
    grp = wg_ref.shape[1]
    y = None
    for gi in range(wg_ref.shape[0]):
        cols = slice(gi * grp, (gi + 1) * grp)
        e = jnp.dot(d_ref[:, cols], wg_ref[gi], preferred_element_type=_f32) * sc_ref[:, cols]
        gate = (e * _silu(z_ref[:, cols].astype(_f32))).astype(_bf16)
        part = jnp.dot(gate, wo_ref[cols, :], preferred_element_type=_f32)
        y = part if y is None else y + part
    o_ref[...] = _layer_norm(ALPHA * h_ref[...] + y, g_ref[...], b_ref[...])


def _pool_tail(dpool, z, h1, w_grp, scale, w_out, g, b, *, tm):
    m, d = h1.shape
    rows = lambda: pl.BlockSpec((tm, d), lambda i: (i, 0))
    return pl.pallas_call(
        _pool_tail_kernel,
        grid=(m // tm,),
        in_specs=[rows(), rows(), rows(), _resident(w_grp.shape), _resident((1, d)),
                  _resident((d, d)), _resident((1, d)), _resident((1, d))],
        out_specs=rows(),
        out_shape=jax.ShapeDtypeStruct((m, d), _f32),
        compiler_params=_params("arbitrary"),
        name="pool_tail",
    )(dpool, z, h1, w_grp, scale, w_out, g, b)


def kernel(x, meta_tokens, fox_w_in, fox_b_f, fox_w_out, ln0_g, ln0_b, pool_w_in, pool_w_grp,
           pool_scale, pool_w_out, ln1_g, ln1_b):
    bsz, seq, d = x.shape
    width = N_HEADS * HEAD_DIM
    m = bsz * seq

    w_in_t = fox_w_in.T.astype(_bf16)
    w_out0 = fox_w_out.astype(_bf16)
    w_pin = pool_w_in.astype(_bf16)
    w_grp = pool_w_grp.astype(_bf16)
    w_out1 = pool_w_out.astype(_bf16)
    row = lambda t: t.reshape(1, d).astype(_f32)

    x2 = x.reshape(m, d)
    meta = meta_tokens.astype(_f32)

    qkvz, fl_seq = _fox_proj(x2, w_in_t, tm=1024, tn=1024)
    qkvz_m, fl_meta = _fox_proj(meta, w_in_t, tm=N_META, tn=1024)

    n_pad = LANES - N_META
    fl = jnp.concatenate(
        [jnp.zeros((bsz, N_HEADS, n_pad), _f32),
         jnp.broadcast_to(fl_meta[:, :N_HEADS].T[None], (bsz, N_HEADS, N_META)),
         fl_seq[:, :N_HEADS].reshape(bsz, seq, N_HEADS).transpose(0, 2, 1)], axis=2)
    c_full = _cum(fl, fox_b_f.reshape(N_HEADS, 1).astype(_f32), n_pad=n_pad)
    c_seq = c_full[:, :, LANES:].reshape(bsz, N_HEADS, 1, seq)
    c_meta = jnp.concatenate(
        [c_full[0, :, n_pad:LANES], jnp.full((N_HEADS, n_pad), NO_KEY, _f32)],
        axis=1).reshape(N_HEADS, 1, LANES)
    qkvz_m128 = jnp.pad(qkvz_m, ((0, LANES - N_META), (0, 0)))

    a = _attention(qkvz.reshape(bsz, seq, 4 * width), qkvz_m, c_seq, c_meta, tile=512)
    a_m = _meta_attention(qkvz_m128, c_meta)

    g0, b0 = row(ln0_g), row(ln0_b)
    h1_m = _out_ln(a_m, w_out0, meta, g0, b0, tm=N_META)
    u_m = _matmul(h1_m, w_pin, col0=0, n=d, out_dtype=_f32, tm=N_META, tn=1024)

    h1, dpool, z1 = _layer_boundary(a.reshape(m, width), w_out0, x2, g0, b0, w_pin, u_m,
                                    tm=256, seq=seq)
    out = _pool_tail(dpool, z1, h1, w_grp, row(pool_scale), w_out1, row(ln1_g), row(ln1_b),
                     tm=256)
    return out.reshape(bsz, seq, d)
```

```python
import functools
import math

import jax
import jax.numpy as jnp
from jax import lax
from jax.experimental import pallas as pl
from jax.experimental.pallas import tpu as pltpu

N_META = 16
N_HEADS = 16
HEAD_DIM = 128
POOL_WINDOWS = (2, 4, 8, 16)
DEPTH = 2
ALPHA = (2.0 * DEPTH) ** 0.25
LN_EPS = 1e-5

LANES = 128
BF16_SUBLANES = 16
MASKED = -1e30
NO_KEY = 1e30
VMEM_LIMIT = 56 * 1024 * 1024
LOG2E = math.log2(math.e)
SOFTMAX_LAG = 2
PV_LAG = 2

_NT = (((1,), (1,)), ((), ()))
_TN = (((0,), (0,)), ((), ()))
_f32 = jnp.float32
_bf16 = jnp.bfloat16


def _params(*sem):
    return pltpu.CompilerParams(dimension_semantics=sem, vmem_limit_bytes=VMEM_LIMIT)


def _fox_proj_kernel(x_ref, w_ref, wf_ref, out_ref, fl_ref, xb_ref, *, n_q_tiles, q_scale):
    j = pl.program_id(1)

    @pl.when(j == 0)
    def _():
        xb = x_ref[...].astype(_bf16)
        xb_ref[...] = xb
        wf = wf_ref[...]
        wf = jnp.concatenate([wf, jnp.zeros((LANES - wf.shape[0], wf.shape[1]), wf.dtype)], axis=0)
        fl_ref[...] = lax.dot_general(xb, wf, _NT, preferred_element_type=_f32)

    acc = lax.dot_general(xb_ref[...], w_ref[...], _NT, preferred_element_type=_f32)
    scale = jnp.where(j < n_q_tiles, q_scale, 1.0).astype(_f32)
    out_ref[...] = (acc * scale).astype(out_ref.dtype)


def _fox_proj(x, w_t, *, tm, tn):
    m, d = x.shape
    n = 4 * N_HEADS * HEAD_DIM
    kern = functools.partial(_fox_proj_kernel, n_q_tiles=(N_HEADS * HEAD_DIM) // tn,
                             q_scale=HEAD_DIM ** -0.5 * LOG2E)
    return pl.pallas_call(
        kern,
        grid=(m // tm, n // tn),
        in_specs=[
            pl.BlockSpec((tm, d), lambda i, j: (i, 0)),
            pl.BlockSpec((tn, d), lambda i, j: (j, 0)),
            pl.BlockSpec((N_HEADS, d), lambda i, j: (n // N_HEADS, 0)),
        ],
        out_specs=[
            pl.BlockSpec((tm, tn), lambda i, j: (i, j)),
            pl.BlockSpec((tm, LANES), lambda i, j: (i, 0)),
        ],
        out_shape=[
            jax.ShapeDtypeStruct((m, n), _bf16),
            jax.ShapeDtypeStruct((m, LANES), _f32),
        ],
        scratch_shapes=[pltpu.VMEM((tm, d), _bf16)],
        compiler_params=_params("arbitrary", "arbitrary"),
        name="fox_proj",
    )(x, w_t, w_t)


def _cum_kernel(fl_ref, bf_ref, c_ref, *, n_pad):
    x = fl_ref[...] + bf_ref[...]
    lf = jnp.minimum(x, 0.0) - jnp.log1p(jnp.exp(-jnp.abs(x)))
    lane = lax.broadcasted_iota(jnp.int32, x.shape, 1)
    lf = jnp.where(lane >= n_pad, lf, 0.0)
    shift = 1
    while shift < x.shape[1]:
        lf = lf + jnp.where(lane >= shift, pltpu.roll(lf, shift, axis=1), 0.0)
        shift *= 2
    c_ref[...] = lf * LOG2E


def _cum(fl, b_f, *, n_pad):
    b, h, l = fl.shape
    return pl.pallas_call(
        functools.partial(_cum_kernel, n_pad=n_pad),
        grid=(b,),
        in_specs=[
            pl.BlockSpec((None, h, l), lambda i: (i, 0, 0)),
            pl.BlockSpec((h, 1), lambda i: (0, 0)),
        ],
        out_specs=pl.BlockSpec((None, h, l), lambda i: (i, 0, 0)),
        out_shape=jax.ShapeDtypeStruct((b, h, l), _f32),
        compiler_params=_params("arbitrary"),
        name="fox_decay_cumsum",
    )(fl, b_f)


def _lane_replicated_column(row):
    return jnp.broadcast_to(row, (LANES, LANES)).T


def _silu(z):
    return z * jax.nn.sigmoid(z)


def _attention_items(n_q, span):
    return [(qi, t, min(span, qi + 1 - t)) for qi in range(n_q) for t in range(0, qi + 1, span)]


def _attn_kernel(q_ref, k_ref, v_ref, z_ref, km_ref, vm_ref, c_ref, cm_ref, o_ref,
                 kaug_sc, cmrep_sc, qt_sc, vt_sc, m_sc, l_sc, acc_sc, *, tile, span):
    n_q = q_ref.shape[0] // tile
    dh = q_ref.shape[1]
    items = _attention_items(n_q, span)
    n_items = len(items)

    def bf16_pieces(x):
        hi = x.astype(_bf16).astype(_f32)
        mid = (x - hi).astype(_bf16).astype(_f32)
        lo = (x - hi - mid).astype(_bf16).astype(_f32)
        return hi, mid, lo

    def stack_rows(shape, rows):
        sub = lax.broadcasted_iota(jnp.int32, shape, 0)
        out = jnp.zeros(shape, _f32)
        for r, row in enumerate(rows):
            out = jnp.where(sub == r, row, out)
        return out

    seq = c_ref.shape[1]
    neg_c = bf16_pieces(-c_ref[...])
    ones = jnp.ones_like(neg_c[0])
    key_terms = stack_rows((dh, seq), [*neg_c, ones, ones, ones])
    kaug_sc[:, 0:dh] = k_ref[...]
    kaug_sc[:, dh:] = key_terms.astype(_bf16).T
    cmrep_sc[...] = _lane_replicated_column(cm_ref[...])

    q_t = q_ref[...].T
    vt_sc[0:dh, :] = v_ref[...].T
    vt_sc[dh:, :] = jnp.ones((vt_sc.shape[0] - dh, seq), _bf16)
    c_shifts = []
    for qi in range(n_q):
        rows = slice(qi * tile, (qi + 1) * tile)
        c_shift = jnp.min(c_ref[:, rows], axis=1, keepdims=True)
        c_shifts.append(jnp.broadcast_to(c_shift, (1, tile)))
        one = jnp.ones((1, tile), _f32)
        shift = [jnp.broadcast_to(piece, (1, tile)) for piece in bf16_pieces(c_shift)]
        qt_sc[qi, 0:dh, :] = q_t[:, rows]
        qt_sc[qi, dh:, :] = stack_rows((dh, tile), [one, one, one, *shift]).astype(_bf16)

    bias = jnp.concatenate(c_shifts, axis=1) - jnp.tile(cmrep_sc[0:N_META, :], (1, seq // LANES))
    s = jnp.dot(km_ref[...], q_t, preferred_element_type=_f32) + bias
    m = jnp.max(s, axis=0, keepdims=True)
    p = jnp.exp2(s - m)
    l = jnp.sum(p, axis=0, keepdims=True)
    acc = lax.dot_general(vm_ref[...], p.astype(_bf16), _TN, preferred_element_type=_f32)
    for qi in range(n_q):
        rows = slice(qi * tile, (qi + 1) * tile)
        m_sc[qi] = m[:, rows]
        l_sc[qi] = l[:, rows]
        acc_sc[qi] = acc[:, rows]

    def key_rows(i):
        _, t, n = items[i]
        return slice(t * tile, (t + n) * tile)

    scores, probs = {}, {}

    def stage_scores(i):
        qi, t, n = items[i]
        s = jnp.dot(kaug_sc[key_rows(i), :], qt_sc[qi], preferred_element_type=_f32)
        if t + n - 1 == qi:
            key = lax.broadcasted_iota(jnp.int32, s.shape, 0)
            qry = lax.broadcasted_iota(jnp.int32, s.shape, 1)
            s = jnp.where(key - qry <= (qi - t) * tile, s, MASKED)
        scores[i] = (s, jnp.max(s, axis=0, keepdims=True))

    def stage_softmax(i):
        qi = items[i][0]
        s, s_max = scores.pop(i)
        m_prev = m_sc[qi]
        m_new = jnp.maximum(m_prev, s_max)
        m_sc[qi] = m_new
        probs[i] = (jnp.exp2(s - m_new).astype(_bf16), jnp.exp2(m_prev - m_new))

    def stage_pv(i):
        qi = items[i][0]
        p, alpha = probs.pop(i)
        pv = jnp.dot(vt_sc[:, key_rows(i)], p, preferred_element_type=_f32)
        acc_sc[qi] = alpha * acc_sc[qi] + pv[0:dh]
        l_sc[qi] = alpha * l_sc[qi] + pv[dh:dh + 1]

    lag1, lag2 = SOFTMAX_LAG, PV_LAG
    for step in range(n_items + lag1 + lag2):
        if step < n_items:
            stage_scores(step)
        if 0 <= step - lag1 < n_items:
            stage_softmax(step - lag1)
        if 0 <= step - lag1 - lag2 < n_items:
            stage_pv(step - lag1 - lag2)

    for qi in range(n_q):
        rows = slice(qi * tile, (qi + 1) * tile)
        o = (acc_sc[qi] * (1.0 / l_sc[qi])).T
        o_ref[rows, :] = (o * _silu(z_ref[rows, :].astype(_f32))).astype(o_ref.dtype)


def _attention(qkvz, qkvz_meta, c, c_meta, *, tile, span):
    b, l, _ = qkvz.shape
    h, dh = N_HEADS, HEAD_DIM
    n_q = l // tile
    head_cols = lambda first: pl.BlockSpec((None, l, dh), lambda bi, hi: (bi, 0, first + hi))
    meta_cols = lambda first: pl.BlockSpec((N_META, dh), lambda bi, hi: (0, first + hi))
    return pl.pallas_call(
        functools.partial(_attn_kernel, tile=tile, span=span),
        grid=(b, h),
        in_specs=[
            head_cols(0), head_cols(h), head_cols(2 * h), head_cols(3 * h),
            meta_cols(h), meta_cols(2 * h),
            pl.BlockSpec((None, None, 1, l), lambda bi, hi: (bi, hi, 0, 0)),
            pl.BlockSpec((None, 1, LANES), lambda bi, hi: (hi, 0, 0)),
        ],
        out_specs=pl.BlockSpec((None, l, dh), lambda bi, hi: (bi, 0, hi)),
        scratch_shapes=[
            pltpu.VMEM((l, 2 * dh), _bf16),
            pltpu.VMEM((LANES, LANES), _f32),
            pltpu.VMEM((n_q, 2 * dh, tile), _bf16),
            pltpu.VMEM((dh + BF16_SUBLANES, l), _bf16),
            pltpu.VMEM((n_q, 1, tile), _f32),
            pltpu.VMEM((n_q, 1, tile), _f32),
            pltpu.VMEM((n_q, dh, tile), _f32),
        ],
        out_shape=jax.ShapeDtypeStruct((b, l, h * dh), _bf16),
        compiler_params=_params("arbitrary", "arbitrary"),
        name="fox_attention",
    )(qkvz, qkvz, qkvz, qkvz, qkvz_meta, qkvz_meta, c, c_meta)


def _meta_attn_kernel(qkvz_ref, c_ref, o_ref):
    w = N_HEADS * HEAD_DIM
    row = lax.broadcasted_iota(jnp.int32, (N_META, LANES), 0)
    col = lax.broadcasted_iota(jnp.int32, (N_META, LANES), 1)
    for hd in range(N_HEADS):
        lo = hd * HEAD_DIM
        q = qkvz_ref[0:N_META, lo:lo + HEAD_DIM]
        k = qkvz_ref[:, w + lo:w + lo + HEAD_DIM]
        v = qkvz_ref[:, 2 * w + lo:2 * w + lo + HEAD_DIM]
        z = qkvz_ref[0:N_META, 3 * w + lo:3 * w + lo + HEAD_DIM]
        ck = c_ref[hd]
        c_shift = jnp.min(jnp.where(col[0:1] < N_META, ck, NO_KEY), axis=1, keepdims=True)
        s = lax.dot_general(q, k, _NT, preferred_element_type=_f32) + (c_shift - ck)
        s = jnp.where(col <= row, s, MASKED)
        m = jnp.max(s, axis=1, keepdims=True)
        p = jnp.exp2(s - m)
        o = jnp.dot(p.astype(_bf16), v, preferred_element_type=_f32)
        o = o / jnp.sum(p, axis=1, keepdims=True)
        o_ref[:, lo:lo + HEAD_DIM] = (o * _silu(z.astype(_f32))).astype(o_ref.dtype)


def _meta_attention(qkvz_meta, c_meta):
    return pl.pallas_call(
        _meta_attn_kernel,
        out_shape=jax.ShapeDtypeStruct((N_META, N_HEADS * HEAD_DIM), _bf16),
        compiler_params=pltpu.CompilerParams(vmem_limit_bytes=VMEM_LIMIT),
        name="fox_meta_attention",
    )(qkvz_meta, c_meta)


def _layer_norm(r, g, b):
    mu = jnp.mean(r, axis=-1, keepdims=True)
    d = r - mu
    var = jnp.mean(d * d, axis=-1, keepdims=True)
    return d * lax.rsqrt(var + LN_EPS) * g + b


def _out_ln_kernel(a_ref, w_ref, res_ref, g_ref, b_ref, o_ref):
    y = jnp.dot(a_ref[...], w_ref[...], preferred_element_type=_f32)
    o_ref[...] = _layer_norm(ALPHA * res_ref[...] + y, g_ref[...], b_ref[...])


def _resident(shape):
    nd = len(shape)
    return pl.BlockSpec(shape, lambda *_: (0,) * nd, pipeline_mode=pl.Buffered(1))


def _out_ln(a, w, res, g, b, *, tm):
    m, d = res.shape
    k = a.shape[1]
    return pl.pallas_call(
        _out_ln_kernel,
        grid=(m // tm,),
        in_specs=[
            pl.BlockSpec((tm, k), lambda i: (i, 0)),
            _resident((k, d)),
            pl.BlockSpec((tm, d), lambda i: (i, 0)),
            _resident((1, d)),
            _resident((1, d)),
        ],
        out_specs=pl.BlockSpec((tm, d), lambda i: (i, 0)),
        out_shape=jax.ShapeDtypeStruct((m, d), _f32),
        compiler_params=_params("arbitrary"),
        name="out_proj_layernorm",
    )(a, w, res, g, b)


def _matmul_kernel(x_ref, w_ref, o_ref, xb_ref):
    @pl.when(pl.program_id(1) == 0)
    def _():
        xb_ref[...] = x_ref[...].astype(_bf16)

    o_ref[...] = jnp.dot(xb_ref[...], w_ref[...], preferred_element_type=_f32).astype(o_ref.dtype)


def _matmul(x, w, *, col0, n, out_dtype, tm, tn):
    m, d = x.shape
    off = col0 // tn
    return pl.pallas_call(
        _matmul_kernel,
        grid=(m // tm, n // tn),
        in_specs=[
            pl.BlockSpec((tm, d), lambda i, j: (i, 0)),
            pl.BlockSpec((d, tn), lambda i, j: (0, off + j)),
        ],
        out_specs=pl.BlockSpec((tm, tn), lambda i, j: (i, j)),
        out_shape=jax.ShapeDtypeStruct((m, n), out_dtype),
        scratch_shapes=[pltpu.VMEM((tm, d), _bf16)],
        compiler_params=_params("arbitrary", "arbitrary"),
        name="pool_in_proj",
    )(x, w)


def _layer_boundary_kernel(a_ref, wo_ref, res_ref, g_ref, b_ref, wi_ref, umeta_ref,
                           h_ref, d_ref, z_ref, ext_ref, *, tm, tiles_per_seq):
    halo = max(POOL_WINDOWS)
    width = d_ref.shape[1]
    first = pl.program_id(0) % tiles_per_seq == 0

    @pl.when(first)
    def _():
        ext_ref[0:halo, :] = umeta_ref[...]

    @pl.when(jnp.logical_not(first))
    def _():
        ext_ref[0:halo, :] = ext_ref[tm:tm + halo, :]

    y = jnp.dot(a_ref[...], wo_ref[...], preferred_element_type=_f32)
    h = _layer_norm(ALPHA * res_ref[...] + y, g_ref[...], b_ref[...])
    h_ref[...] = h
    hb = h.astype(_bf16)
    ext_ref[halo:halo + tm, :] = jnp.dot(hb, wi_ref[:, 0:width], preferred_element_type=_f32)
    z_ref[...] = jnp.dot(hb, wi_ref[:, width:2 * width],
                         preferred_element_type=_f32).astype(z_ref.dtype)

    grp = width // len(POOL_WINDOWS)
    for gi, win in enumerate(POOL_WINDOWS):
        cols = slice(gi * grp, (gi + 1) * grp)
        u = ext_ref[:, cols]
        tot, span = u, 1
        while span < win:
            tot = tot + pltpu.roll(tot, span, axis=0)
            span *= 2
        d = tot[halo:] * (1.0 / win) - u[halo:]
        d_ref[:, cols] = d.astype(d_ref.dtype)


def _layer_boundary(a, w_out, res, g, b, w_in, u_meta, *, tm, seq):
    m, d = res.shape
    halo = max(POOL_WINDOWS)
    rows = lambda: pl.BlockSpec((tm, d), lambda i: (i, 0))
    kern = functools.partial(_layer_boundary_kernel, tm=tm, tiles_per_seq=seq // tm)
    return pl.pallas_call(
        kern,
        grid=(m // tm,),
        in_specs=[rows(), _resident(w_out.shape), rows(), _resident((1, d)), _resident((1, d)),
                  _resident(w_in.shape), _resident((halo, d))],
        out_specs=[rows(), rows(), rows()],
        out_shape=[jax.ShapeDtypeStruct((m, d), _f32),
                   jax.ShapeDtypeStruct((m, d), _bf16),
                   jax.ShapeDtypeStruct((m, d), _bf16)],
        scratch_shapes=[pltpu.VMEM((tm + halo, d), _f32)],
        compiler_params=_params("arbitrary"),
        name="layer_boundary",
    )(a, w_out, res, g, b, w_in, u_meta)


def _pool_tail_kernel(d_ref, z_ref, h_ref, wg_ref, sc_ref, wo_ref, g_ref, b_ref, o_ref, *, sub):
    grp = wg_ref.shape[1]

    def project(rows):
        y = None
        for gi in range(wg_ref.shape[0]):
            cols = slice(gi * grp, (gi + 1) * grp)
            e = jnp.dot(d_ref[rows, cols], wg_ref[gi], preferred_element_type=_f32) * sc_ref[:, cols]
            gate = (e * _silu(z_ref[rows, cols].astype(_f32))).astype(_bf16)
            part = jnp.dot(gate, wo_ref[cols, :], preferred_element_type=_f32)
            y = part if y is None else y + part
        return y

    def finish(rows, y):
        o_ref[rows, :] = _layer_norm(ALPHA * h_ref[rows, :] + y, g_ref[...], b_ref[...])

    n_sub = o_ref.shape[0] // sub
    rows = lambda s: slice(s * sub, (s + 1) * sub)
    pending = {}
    for s in range(n_sub + 1):
        if s < n_sub:
            pending[s] = project(rows(s))
        if s >= 1:
            finish(rows(s - 1), pending.pop(s - 1))


def _pool_tail(dpool, z, h1, w_grp, scale, w_out, g, b, *, tm, sub):
    m, d = h1.shape
    rows = lambda: pl.BlockSpec((tm, d), lambda i: (i, 0))
    return pl.pallas_call(
        functools.partial(_pool_tail_kernel, sub=sub),
        grid=(m // tm,),
        in_specs=[rows(), rows(), rows(), _resident(w_grp.shape), _resident((1, d)),
                  _resident((d, d)), _resident((1, d)), _resident((1, d))],
        out_specs=rows(),
        out_shape=jax.ShapeDtypeStruct((m, d), _f32),
        compiler_params=_params("arbitrary"),
        name="pool_tail",
    )(dpool, z, h1, w_grp, scale, w_out, g, b)


def kernel(x, meta_tokens, fox_w_in, fox_b_f, fox_w_out, ln0_g, ln0_b, pool_w_in, pool_w_grp,
           pool_scale, pool_w_out, ln1_g, ln1_b):
    bsz, seq, d = x.shape
    width = N_HEADS * HEAD_DIM
    m = bsz * seq

    w_in_t = fox_w_in.T.astype(_bf16)
    w_out0 = fox_w_out.astype(_bf16)
    w_pin = pool_w_in.astype(_bf16)
    w_grp = pool_w_grp.astype(_bf16)
    w_out1 = pool_w_out.astype(_bf16)
    row = lambda t: t.reshape(1, d).astype(_f32)

    x2 = x.reshape(m, d)
    meta = meta_tokens.astype(_f32)

    qkvz, fl_seq = _fox_proj(x2, w_in_t, tm=1024, tn=1024)
    qkvz_m, fl_meta = _fox_proj(meta, w_in_t, tm=N_META, tn=1024)

    n_pad = LANES - N_META
    fl = jnp.concatenate(
        [jnp.zeros((bsz, N_HEADS, n_pad), _f32),
         jnp.broadcast_to(fl_meta[:, :N_HEADS].T[None], (bsz, N_HEADS, N_META)),
         fl_seq[:, :N_HEADS].reshape(bsz, seq, N_HEADS).transpose(0, 2, 1)], axis=2)
    c_full = _cum(fl, fox_b_f.reshape(N_HEADS, 1).astype(_f32), n_pad=n_pad)
    c_seq = c_full[:, :, LANES:].reshape(bsz, N_HEADS, 1, seq)
    c_meta = jnp.concatenate(
        [c_full[0, :, n_pad:LANES], jnp.full((N_HEADS, n_pad), NO_KEY, _f32)],
        axis=1).reshape(N_HEADS, 1, LANES)
    qkvz_m128 = jnp.pad(qkvz_m, ((0, LANES - N_META), (0, 0)))

    a = _attention(qkvz.reshape(bsz, seq, 4 * width), qkvz_m, c_seq, c_meta, tile=512, span=2)
    a_m = _meta_attention(qkvz_m128, c_meta)

    g0, b0 = row(ln0_g), row(ln0_b)
    h1_m = _out_ln(a_m, w_out0, meta, g0, b0, tm=N_META)
    u_m = _matmul(h1_m, w_pin, col0=0, n=d, out_dtype=_f32, tm=N_META, tn=1024)

    h1, dpool, z1 = _layer_boundary(a.reshape(m, width), w_out0, x2, g0, b0, w_pin, u_m,
                                    tm=256, seq=seq)
    out = _pool_tail(dpool, z1, h1, w_grp, row(pool_scale), w_out1, row(ln1_g), row(ln1_b),
                     tm=512, sub=256)
    return out.reshape(bsz, seq, d)
```

```python
import functools
import math

import jax
import jax.numpy as jnp
from jax import lax
from jax.experimental import pallas as pl
from jax.experimental.pallas import tpu as pltpu

N_META = 16
N_HEADS = 16
HEAD_DIM = 128
POOL_WINDOWS = (2, 4, 8, 16)
DEPTH = 2
ALPHA = (2.0 * DEPTH) ** 0.25
LN_EPS = 1e-5

LANES = 128
BF16_SUBLANES = 16
MASKED = -1e30
NO_KEY = 1e30
VMEM_LIMIT = 56 * 1024 * 1024
LOG2E = math.log2(math.e)
SOFTMAX_LAG = 2
PV_LAG = 2

_NT = (((1,), (1,)), ((), ()))
_TN = (((0,), (0,)), ((), ()))
_f32 = jnp.float32
_bf16 = jnp.bfloat16


def _params(*sem):
    return pltpu.CompilerParams(dimension_semantics=sem, vmem_limit_bytes=VMEM_LIMIT)


def _fox_proj_kernel(x_ref, w_ref, wf_ref, out_ref, fl_ref, xb_ref, *, n_q_tiles, q_scale):
    j = pl.program_id(1)

    @pl.when(j == 0)
    def _():
        xb = x_ref[...].astype(_bf16)
        xb_ref[...] = xb
        wf = wf_ref[...]
        wf = jnp.concatenate([wf, jnp.zeros((LANES - wf.shape[0], wf.shape[1]), wf.dtype)], axis=0)
        fl_ref[...] = lax.dot_general(xb, wf, _NT, preferred_element_type=_f32)

    acc = lax.dot_general(xb_ref[...], w_ref[...], _NT, preferred_element_type=_f32)
    scale = jnp.where(j < n_q_tiles, q_scale, 1.0).astype(_f32)
    out_ref[...] = (acc * scale).astype(out_ref.dtype)


def _fox_proj(x, w_t, *, tm, tn):
    m, d = x.shape
    n = 4 * N_HEADS * HEAD_DIM
    kern = functools.partial(_fox_proj_kernel, n_q_tiles=(N_HEADS * HEAD_DIM) // tn,
                             q_scale=HEAD_DIM ** -0.5 * LOG2E)
    return pl.pallas_call(
        kern,
        grid=(m // tm, n // tn),
        in_specs=[
            pl.BlockSpec((tm, d), lambda i, j: (i, 0)),
            pl.BlockSpec((tn, d), lambda i, j: (j, 0)),
            pl.BlockSpec((N_HEADS, d), lambda i, j: (n // N_HEADS, 0)),
        ],
        out_specs=[
            pl.BlockSpec((tm, tn), lambda i, j: (i, j)),
            pl.BlockSpec((tm, LANES), lambda i, j: (i, 0)),
        ],
        out_shape=[
            jax.ShapeDtypeStruct((m, n), _bf16),
            jax.ShapeDtypeStruct((m, LANES), _f32),
        ],
        scratch_shapes=[pltpu.VMEM((tm, d), _bf16)],
        compiler_params=_params("arbitrary", "arbitrary"),
        name="fox_proj",
    )(x, w_t, w_t)


def _cum_kernel(fl_ref, bf_ref, c_ref, *, n_pad):
    x = fl_ref[...] + bf_ref[...]
    lf = jnp.minimum(x, 0.0) - jnp.log1p(jnp.exp(-jnp.abs(x)))
    lane = lax.broadcasted_iota(jnp.int32, x.shape, 1)
    lf = jnp.where(lane >= n_pad, lf, 0.0)
    shift = 1
    while shift < x.shape[1]:
        lf = lf + jnp.where(lane >= shift, pltpu.roll(lf, shift, axis=1), 0.0)
        shift *= 2
    c_ref[...] = lf * LOG2E


def _cum(fl, b_f, *, n_pad):
    b, h, l = fl.shape
    return pl.pallas_call(
        functools.partial(_cum_kernel, n_pad=n_pad),
        grid=(b,),
        in_specs=[
            pl.BlockSpec((None, h, l), lambda i: (i, 0, 0)),
            pl.BlockSpec((h, 1), lambda i: (0, 0)),
        ],
        out_specs=pl.BlockSpec((None, h, l), lambda i: (i, 0, 0)),
        out_shape=jax.ShapeDtypeStruct((b, h, l), _f32),
        compiler_params=_params("arbitrary"),
        name="fox_decay_cumsum",
    )(fl, b_f)


def _lane_replicated_column(row):
    return jnp.broadcast_to(row, (LANES, LANES)).T


def _silu(z):
    return z * jax.nn.sigmoid(z)


def _attn_kernel(q_ref, k_ref, v_ref, z_ref, km_ref, vm_ref, c_ref, cm_ref, o_ref,
                 kaug_sc, cmrep_sc, qt_sc, vt_sc, *, tile):
    n_q = q_ref.shape[0] // tile
    dh = q_ref.shape[1]

    def bf16_pieces(x):
        hi = x.astype(_bf16).astype(_f32)
        mid = (x - hi).astype(_bf16).astype(_f32)
        lo = (x - hi - mid).astype(_bf16).astype(_f32)
        return hi, mid, lo

    def stack_rows(shape, rows):
        sub = lax.broadcasted_iota(jnp.int32, shape, 0)
        out = jnp.zeros(shape, _f32)
        for r, row in enumerate(rows):
            out = jnp.where(sub == r, row, out)
        return out

    seq = c_ref.shape[1]
    neg_c = bf16_pieces(-c_ref[...])
    ones = jnp.ones_like(neg_c[0])
    key_terms = stack_rows((dh, seq), [*neg_c, ones, ones, ones])
    kaug_sc[:, 0:dh] = k_ref[...]
    kaug_sc[:, dh:] = key_terms.astype(_bf16).T
    cmrep_sc[...] = _lane_replicated_column(cm_ref[...])

    q_t = q_ref[...].T
    vt_sc[0:dh, :] = v_ref[...].T
    vt_sc[dh:, :] = jnp.ones((vt_sc.shape[0] - dh, seq), _bf16)
    c_shifts = []
    for qi in range(n_q):
        rows = slice(qi * tile, (qi + 1) * tile)
        c_shift = jnp.min(c_ref[:, rows], axis=1, keepdims=True)
        c_shifts.append(jnp.broadcast_to(c_shift, (1, tile)))
        one = jnp.ones((1, tile), _f32)
        shift = [jnp.broadcast_to(piece, (1, tile)) for piece in bf16_pieces(c_shift)]
        qt_sc[qi, 0:dh, :] = q_t[:, rows]
        qt_sc[qi, dh:, :] = stack_rows((dh, tile), [one, one, one, *shift]).astype(_bf16)

    bias = jnp.concatenate(c_shifts, axis=1) - jnp.tile(cmrep_sc[0:N_META, :], (1, seq // LANES))
    s = jnp.dot(km_ref[...], q_t, preferred_element_type=_f32) + bias
    m = jnp.max(s, axis=0, keepdims=True)
    p = jnp.exp2(s - m)
    l = jnp.sum(p, axis=0, keepdims=True)
    acc = lax.dot_general(vm_ref[...], p.astype(_bf16), _TN, preferred_element_type=_f32)
    meta_state = (m, l, acc)

    scores, probs = {}, {}

    def stage_scores(qi):
        keys = (qi + 1) * tile
        s = jnp.dot(kaug_sc[0:keys, :], qt_sc[qi], preferred_element_type=_f32)
        key = lax.broadcasted_iota(jnp.int32, (tile, tile), 0)
        qry = lax.broadcasted_iota(jnp.int32, (tile, tile), 1)
        diag = jnp.where(key <= qry, s[keys - tile:], MASKED)
        s = diag if qi == 0 else jnp.concatenate([s[0:keys - tile], diag], axis=0)
        scores[qi] = (s, jnp.max(s, axis=0, keepdims=True))

    def stage_softmax(qi):
        s, s_max = scores.pop(qi)
        m_meta = meta_state[0][:, qi * tile:(qi + 1) * tile]
        m_all = jnp.maximum(m_meta, s_max)
        probs[qi] = (jnp.exp2(s - m_all).astype(_bf16), jnp.exp2(m_meta - m_all))

    def stage_pv(qi):
        rows = slice(qi * tile, (qi + 1) * tile)
        p, alpha = probs.pop(qi)
        pv = jnp.dot(vt_sc[:, 0:(qi + 1) * tile], p, preferred_element_type=_f32)
        acc_all = alpha * meta_state[2][:, rows] + pv[0:dh]
        l_all = alpha * meta_state[1][:, rows] + pv[dh:dh + 1]
        o = (acc_all * (1.0 / l_all)).T
        o_ref[rows, :] = (o * _silu(z_ref[rows, :].astype(_f32))).astype(o_ref.dtype)

    lag1, lag2 = SOFTMAX_LAG, PV_LAG
    for step in range(n_q + lag1 + lag2):
        if step < n_q:
            stage_scores(step)
        if 0 <= step - lag1 < n_q:
            stage_softmax(step - lag1)
        if 0 <= step - lag1 - lag2 < n_q:
            stage_pv(step - lag1 - lag2)


def _attention(qkvz, qkvz_meta, c, c_meta, *, tile):
    b, l, _ = qkvz.shape
    h, dh = N_HEADS, HEAD_DIM
    n_q = l // tile
    head_cols = lambda first: pl.BlockSpec((None, l, dh), lambda bi, hi: (bi, 0, first + hi))
    meta_cols = lambda first: pl.BlockSpec((N_META, dh), lambda bi, hi: (0, first + hi))
    return pl.pallas_call(
        functools.partial(_attn_kernel, tile=tile),
        grid=(b, h),
        in_specs=[
            head_cols(0), head_cols(h), head_cols(2 * h), head_cols(3 * h),
            meta_cols(h), meta_cols(2 * h),
            pl.BlockSpec((None, None, 1, l), lambda bi, hi: (bi, hi, 0, 0)),
            pl.BlockSpec((None, 1, LANES), lambda bi, hi: (hi, 0, 0)),
        ],
        out_specs=pl.BlockSpec((None, l, dh), lambda bi, hi: (bi, 0, hi)),
        scratch_shapes=[
            pltpu.VMEM((l, 2 * dh), _bf16),
            pltpu.VMEM((LANES, LANES), _f32),
            pltpu.VMEM((n_q, 2 * dh, tile), _bf16),
            pltpu.VMEM((dh + BF16_SUBLANES, l), _bf16),
        ],
        out_shape=jax.ShapeDtypeStruct((b, l, h * dh), _bf16),
        compiler_params=_params("arbitrary", "arbitrary"),
        name="fox_attention",
    )(qkvz, qkvz, qkvz, qkvz, qkvz_meta, qkvz_meta, c, c_meta)


def _meta_attn_kernel(qkvz_ref, c_ref, o_ref):
    w = N_HEADS * HEAD_DIM
    row = lax.broadcasted_iota(jnp.int32, (N_META, LANES), 0)
    col = lax.broadcasted_iota(jnp.int32, (N_META, LANES), 1)
    for hd in range(N_HEADS):
        lo = hd * HEAD_DIM
        q = qkvz_ref[0:N_META, lo:lo + HEAD_DIM]
        k = qkvz_ref[:, w + lo:w + lo + HEAD_DIM]
        v = qkvz_ref[:, 2 * w + lo:2 * w + lo + HEAD_DIM]
        z = qkvz_ref[0:N_META, 3 * w + lo:3 * w + lo + HEAD_DIM]
        ck = c_ref[hd]
        c_shift = jnp.min(jnp.where(col[0:1] < N_META, ck, NO_KEY), axis=1, keepdims=True)
        s = lax.dot_general(q, k, _NT, preferred_element_type=_f32) + (c_shift - ck)
        s = jnp.where(col <= row, s, MASKED)
        m = jnp.max(s, axis=1, keepdims=True)
        p = jnp.exp2(s - m)
        o = jnp.dot(p.astype(_bf16), v, preferred_element_type=_f32)
        o = o / jnp.sum(p, axis=1, keepdims=True)
        o_ref[:, lo:lo + HEAD_DIM] = (o * _silu(z.astype(_f32))).astype(o_ref.dtype)


def _meta_attention(qkvz_meta, c_meta):
    return pl.pallas_call(
        _meta_attn_kernel,
        out_shape=jax.ShapeDtypeStruct((N_META, N_HEADS * HEAD_DIM), _bf16),
        compiler_params=pltpu.CompilerParams(vmem_limit_bytes=VMEM_LIMIT),
        name="fox_meta_attention",
    )(qkvz_meta, c_meta)


def _layer_norm(r, g, b):
    mu = jnp.mean(r, axis=-1, keepdims=True)
    d = r - mu
    var = jnp.mean(d * d, axis=-1, keepdims=True)
    return d * lax.rsqrt(var + LN_EPS) * g + b


def _out_ln_kernel(a_ref, w_ref, res_ref, g_ref, b_ref, o_ref):
    y = jnp.dot(a_ref[...], w_ref[...], preferred_element_type=_f32)
    o_ref[...] = _layer_norm(ALPHA * res_ref[...] + y, g_ref[...], b_ref[...])


def _resident(shape):
    nd = len(shape)
    return pl.BlockSpec(shape, lambda *_: (0,) * nd, pipeline_mode=pl.Buffered(1))


def _out_ln(a, w, res, g, b, *, tm):
    m, d = res.shape
    k = a.shape[1]
    return pl.pallas_call(
        _out_ln_kernel,
        grid=(m // tm,),
        in_specs=[
            pl.BlockSpec((tm, k), lambda i: (i, 0)),
            _resident((k, d)),
            pl.BlockSpec((tm, d), lambda i: (i, 0)),
            _resident((1, d)),
            _resident((1, d)),
        ],
        out_specs=pl.BlockSpec((tm, d), lambda i: (i, 0)),
        out_shape=jax.ShapeDtypeStruct((m, d), _f32),
        compiler_params=_params("arbitrary"),
        name="out_proj_layernorm",
    )(a, w, res, g, b)


def _matmul_kernel(x_ref, w_ref, o_ref, xb_ref):
    @pl.when(pl.program_id(1) == 0)
    def _():
        xb_ref[...] = x_ref[...].astype(_bf16)

    o_ref[...] = jnp.dot(xb_ref[...], w_ref[...], preferred_element_type=_f32).astype(o_ref.dtype)


def _matmul(x, w, *, col0, n, out_dtype, tm, tn):
    m, d = x.shape
    off = col0 // tn
    return pl.pallas_call(
        _matmul_kernel,
        grid=(m // tm, n // tn),
        in_specs=[
            pl.BlockSpec((tm, d), lambda i, j: (i, 0)),
            pl.BlockSpec((d, tn), lambda i, j: (0, off + j)),
        ],
        out_specs=pl.BlockSpec((tm, tn), lambda i, j: (i, j)),
        out_shape=jax.ShapeDtypeStruct((m, n), out_dtype),
        scratch_shapes=[pltpu.VMEM((tm, d), _bf16)],
        compiler_params=_params("arbitrary", "arbitrary"),
        name="pool_in_proj",
    )(x, w)


def _layer_boundary_kernel(a_ref, wo_ref, res_ref, g_ref, b_ref, wi_ref, umeta_ref,
                           h_ref, d_ref, z_ref, ext_ref, *, tm, tiles_per_seq):
    halo = max(POOL_WINDOWS)
    width = d_ref.shape[1]
    first = pl.program_id(0) % tiles_per_seq == 0

    @pl.when(first)
    def _():
        ext_ref[0:halo, :] = umeta_ref[...]

    @pl.when(jnp.logical_not(first))
    def _():
        ext_ref[0:halo, :] = ext_ref[tm:tm + halo, :]

    y = jnp.dot(a_ref[...], wo_ref[...], preferred_element_type=_f32)
    h = _layer_norm(ALPHA * res_ref[...] + y, g_ref[...], b_ref[...])
    h_ref[...] = h
    hb = h.astype(_bf16)
    ext_ref[halo:halo + tm, :] = jnp.dot(hb, wi_ref[:, 0:width], preferred_element_type=_f32)
    z_ref[...] = jnp.dot(hb, wi_ref[:, width:2 * width],
                         preferred_element_type=_f32).astype(z_ref.dtype)

    grp = width // len(POOL_WINDOWS)
    for gi, win in enumerate(POOL_WINDOWS):
        cols = slice(gi * grp, (gi + 1) * grp)
        u = ext_ref[:, cols]
        tot, span = u, 1
        while span < win:
            tot = tot + pltpu.roll(tot, span, axis=0)
            span *= 2
        d = tot[halo:] * (1.0 / win) - u[halo:]
        d_ref[:, cols] = d.astype(d_ref.dtype)


def _layer_boundary(a, w_out, res, g, b, w_in, u_meta, *, tm, seq):
    m, d = res.shape
    halo = max(POOL_WINDOWS)
    rows = lambda: pl.BlockSpec((tm, d), lambda i: (i, 0))
    kern = functools.partial(_layer_boundary_kernel, tm=tm, tiles_per_seq=seq // tm)
    return pl.pallas_call(
        kern,
        grid=(m // tm,),
        in_specs=[rows(), _resident(w_out.shape), rows(), _resident((1, d)), _resident((1, d)),
                  _resident(w_in.shape), _resident((halo, d))],
        out_specs=[rows(), rows(), rows()],
        out_shape=[jax.ShapeDtypeStruct((m, d), _f32),
                   jax.ShapeDtypeStruct((m, d), _bf16),
                   jax.ShapeDtypeStruct((m, d), _bf16)],
        scratch_shapes=[pltpu.VMEM((tm + halo, d), _f32)],
        compiler_params=_params("arbitrary"),
        name="layer_boundary",
    )(a, w_out, res, g, b, w_in, u_meta)


def _pool_tail_kernel(d_ref, z_ref, h_ref, wg_ref, sc_ref, wo_ref, g_ref, b_ref, o_ref, *, sub):
    grp = wg_ref.shape[1]

    def project(rows):
        y = None
        for gi in range(wg_ref.shape[0]):
            cols = slice(gi * grp, (gi + 1) * grp)
            e = jnp.dot(d_ref[rows, cols], wg_ref[gi], preferred_element_type=_f32) * sc_ref[:, cols]
            gate = (e * _silu(z_ref[rows, cols].astype(_f32))).astype(_bf16)
            part = jnp.dot(gate, wo_ref[cols, :], preferred_element_type=_f32)
            y = part if y is None else y + part
        return y

    def finish(rows, y):
        o_ref[rows, :] = _layer_norm(ALPHA * h_ref[rows, :] + y, g_ref[...], b_ref[...])

    n_sub = o_ref.shape[0] // sub
    rows = lambda s: slice(s * sub, (s + 1) * sub)
    pending = {}
    for s in range(n_sub + 1):
        if s < n_sub:
            pending[s] = project(rows(s))
        if s >= 1:
            finish(rows(s - 1), pending.pop(s - 1))


def _pool_tail(dpool, z, h1, w_grp, scale, w_out, g, b, *, tm, sub):
    m, d = h1.shape
    rows = lambda: pl.BlockSpec((tm, d), lambda i: (i, 0))
    return pl.pallas_call(
        functools.partial(_pool_tail_kernel, sub=sub),
        grid=(m // tm,),
        in_specs=[rows(), rows(), rows(), _resident(w_grp.shape), _resident((1, d)),
                  _resident((d, d)), _resident((1, d)), _resident((1, d))],
        out_specs=rows(),
        out_shape=jax.ShapeDtypeStruct((m, d), _f32),
        compiler_params=_params("arbitrary"),
        name="pool_tail",
    )(dpool, z, h1, w_grp, scale, w_out, g, b)


def kernel(x, meta_tokens, fox_w_in, fox_b_f, fox_w_out, ln0_g, ln0_b, pool_w_in, pool_w_grp,
           pool_scale, pool_w_out, ln1_g, ln1_b):
    bsz, seq, d = x.shape
    width = N_HEADS * HEAD_DIM
    m = bsz * seq

    w_in_t = fox_w_in.T.astype(_bf16)
    w_out0 = fox_w_out.astype(_bf16)
    w_pin = pool_w_in.astype(_bf16)
    w_grp = pool_w_grp.astype(_bf16)
    w_out1 = pool_w_out.astype(_bf16)
    row = lambda t: t.reshape(1, d).astype(_f32)

    x2 = x.reshape(m, d)
    meta = meta_tokens.astype(_f32)

    qkvz, fl_seq = _fox_proj(x2, w_in_t, tm=1024, tn=1024)
    qkvz_m, fl_meta = _fox_proj(meta, w_in_t, tm=N_META, tn=1024)

    n_pad = LANES - N_META
    fl = jnp.concatenate(
        [jnp.zeros((bsz, N_HEADS, n_pad), _f32),
         jnp.broadcast_to(fl_meta[:, :N_HEADS].T[None], (bsz, N_HEADS, N_META)),
         fl_seq[:, :N_HEADS].reshape(bsz, seq, N_HEADS).transpose(0, 2, 1)], axis=2)
    c_full = _cum(fl, fox_b_f.reshape(N_HEADS, 1).astype(_f32), n_pad=n_pad)
    c_seq = c_full[:, :, LANES:].reshape(bsz, N_HEADS, 1, seq)
    c_meta = jnp.concatenate(
        [c_full[0, :, n_pad:LANES], jnp.full((N_HEADS, n_pad), NO_KEY, _f32)],
        axis=1).reshape(N_HEADS, 1, LANES)
    qkvz_m128 = jnp.pad(qkvz_m, ((0, LANES - N_META), (0, 0)))

    a = _attention(qkvz.reshape(bsz, seq, 4 * width), qkvz_m, c_seq, c_meta, tile=512)
    a_m = _meta_attention(qkvz_m128, c_meta)

    g0, b0 = row(ln0_g), row(ln0_b)
    h1_m = _out_ln(a_m, w_out0, meta, g0, b0, tm=N_META)
    u_m = _matmul(h1_m, w_pin, col0=0, n=d, out_dtype=_f32, tm=N_META, tn=1024)

    h1, dpool, z1 = _layer_boundary(a.reshape(m, width), w_out0, x2, g0, b0, w_pin, u_m,
                                    tm=256, seq=seq)
    out = _pool_tail(dpool, z1, h1, w_grp, row(pool_scale), w_out1, row(ln1_g), row(ln1_b),
                     tm=512, sub=256)
    return out.reshape(bsz, seq, d)
```
